```python
import math
import jax, jax.numpy as jnp
from jax import lax
import numpy as np

D_MODEL = 1024
BATCH = 16
SEQ = 2048
DEPTH = 1

HEAD_DIM = 64
ROPE_THETA = 10000.0
EPS = 1e-6
NEG_INF = -1e30
DSW_WINDOWS = (128, 512, 2048)
DSW_DILATIONS = (1, 4, 16)
DSW_GROUPS = 3
DSW_HEADS_PER_GROUP = D_MODEL // 128
DSW_HEADS = DSW_GROUPS * DSW_HEADS_PER_GROUP
DSW_QKV_WIDTH = DSW_HEADS * HEAD_DIM
DSW_OUT_WIDTH = DSW_HEADS_PER_GROUP * HEAD_DIM
DIFF_HEADS = D_MODEL // (2 * HEAD_DIM)
DIFF_QK_WIDTH = DIFF_HEADS * 2 * HEAD_DIM
DIFF_V_WIDTH = DIFF_HEADS * 2 * HEAD_DIM
Q_BLOCK = 128
IN_WIDTHS = (DSW_QKV_WIDTH, DSW_QKV_WIDTH, DSW_QKV_WIDTH, DSW_OUT_WIDTH,
             DIFF_QK_WIDTH, DIFF_QK_WIDTH, DIFF_V_WIDTH, DIFF_V_WIDTH,
             2 * D_MODEL)
IN_WIDTH = sum(IN_WIDTHS)

kernel_name = "hybrid_dilated_diff_attention_block"


def rms_norm(x, w):
    xf = x.astype(jnp.float32)
    y = xf * lax.rsqrt(jnp.mean(xf * xf, axis=-1, keepdims=True) + EPS)
    return (y * w.astype(jnp.float32)).astype(x.dtype)


def rope(x, positions):
    B, S = positions.shape
    half = x.shape[-1] // 2
    inv_freq = ROPE_THETA ** (-jnp.arange(half, dtype=jnp.float32) / half)
    ang = positions.astype(jnp.float32)[..., None] * inv_freq
    ang = ang.reshape((B, S) + (1,) * (x.ndim - 3) + (half,))
    cos, sin = jnp.cos(ang), jnp.sin(ang)
    xf = x.astype(jnp.float32)
    x1, x2 = xf[..., :half], xf[..., half:]
    out = jnp.concatenate([x1 * cos - x2 * sin, x2 * cos + x1 * sin], axis=-1)
    return out.astype(x.dtype)


def dilated_window_attention(q, k, v, window, dilation):
    B, S, H, Dh = q.shape
    r = window // (2 * dilation)
    L = S // dilation
    nb = -(-L // r)
    Lp = nb * r

    def to_blocks(t):
        t = t.reshape(B, L, dilation, H, Dh)
        t = jnp.pad(t, ((0, 0), (0, Lp - L), (0, 0), (0, 0), (0, 0)))
        return t.reshape(B, nb, r, dilation, H, Dh)

    def band(t):
        tp = jnp.pad(t, ((0, 0), (1, 1), (0, 0), (0, 0), (0, 0), (0, 0)))
        return jnp.concatenate([tp[:, :-2], tp[:, 1:-1], tp[:, 2:]], axis=2)

    qb = to_blocks(q)
    kw = band(to_blocks(k))
    vw = band(to_blocks(v))
    s = jnp.einsum('bnishe,bnjshe->bnshij', qb, kw).astype(jnp.float32) * (Dh ** -0.5)
    i = jnp.arange(r)[:, None]
    j = jnp.arange(3 * r)[None, :]
    lk = jnp.arange(nb)[:, None, None] * r + j[None] - r
    valid = (jnp.abs(j - r - i) <= r)[None] & (lk >= 0) & (lk < L)
    s = jnp.where(valid[None, :, None, None], s, NEG_INF)
    m = jnp.max(s, axis=-1, keepdims=True)
    p = jnp.exp(s - m)
    den = jnp.sum(p, axis=-1, keepdims=True)
    o = jnp.einsum('bnshij,bnjshe->bnishe', (p / den).astype(v.dtype), vw)
    lse = (m + jnp.log(den))[..., 0]
    o = o.reshape(B, Lp, dilation, H, Dh)[:, :L].reshape(B, S, H, Dh)
    lse = jnp.transpose(lse, (0, 1, 4, 2, 3)).reshape(B, Lp, dilation, H)[:, :L].reshape(B, S, H)
    return o, lse


def differential_attention(q, k, v, lam):
    B, S, H, _, Dh = q.shape
    nq = S // Q_BLOCK
    qblocks = jnp.moveaxis(q.reshape(B, nq, Q_BLOCK, H, 2, Dh), 1, 0)

    def one_block(qblk):
        s = jnp.einsum('bihce,bjhce->bhcij', qblk, k).astype(jnp.float32) * (Dh ** -0.5)
        p = jax.nn.softmax(s, axis=-1)
        a = p[:, :, 0] - lam * p[:, :, 1]
        return jnp.einsum('bhij,bjhe->bihe', a.astype(v.dtype), v)

    o = lax.map(one_block, qblocks)
    return jnp.moveaxis(o, 0, 1).reshape(B, S, H, 2 * Dh)


def hybrid_layer(x, c, positions, w_ada, b_ada, norm_pre, w_in, lambda_q1, lambda_k1,
                 lambda_q2, lambda_k2, diff_subln, w_proj_a, w_proj_b, w_out, norm_post,
                 lambda_init):
    B, S, D = x.shape
    mod = jax.nn.silu(c) @ w_ada + b_ada
    shift, scale, gate = jnp.split(mod, 3, axis=-1)
    h = rms_norm(x, norm_pre) * (1.0 + scale[:, None]) + shift[:, None]
    proj = h @ w_in
    points, acc = [], 0
    for wdt in IN_WIDTHS[:-1]:
        acc += wdt
        points.append(acc)
    qa, ka, va, za, qd, kd, vd, zd, gm = jnp.split(proj, points, axis=-1)

    qa = rope(qa.reshape(B, S, DSW_HEADS, HEAD_DIM), positions)
    ka = rope(ka.reshape(B, S, DSW_HEADS, HEAD_DIM), positions)
    va = va.reshape(B, S, DSW_HEADS, HEAD_DIM)
    outs, lses = [], []
    for g, (win, dil) in enumerate(zip(DSW_WINDOWS, DSW_DILATIONS)):
        sl = slice(g * DSW_HEADS_PER_GROUP, (g + 1) * DSW_HEADS_PER_GROUP)
        o_g, lse_g = dilated_window_attention(qa[:, :, sl], ka[:, :, sl], va[:, :, sl], win, dil)
        outs.append(o_g)
        lses.append(lse_g)
    alpha = jax.nn.softmax(jnp.stack(lses), axis=0)
    ya = jnp.sum(alpha[..., None] * jnp.stack(outs).astype(jnp.float32), axis=0)
    ya = ya.astype(x.dtype).reshape(B, S, DSW_OUT_WIDTH) * jax.nn.silu(za)
    pa = ya @ w_proj_a

    qd = rope(qd.reshape(B, S, DIFF_HEADS, 2, HEAD_DIM), positions)
    kd = rope(kd.reshape(B, S, DIFF_HEADS, 2, HEAD_DIM), positions)
    vd = vd.reshape(B, S, DIFF_HEADS, 2 * HEAD_DIM)
    f32 = jnp.float32
    lam = (jnp.exp(jnp.sum(lambda_q1.astype(f32) * lambda_k1.astype(f32)))
           - jnp.exp(jnp.sum(lambda_q2.astype(f32) * lambda_k2.astype(f32))) + lambda_init)
    yd = differential_attention(qd, kd, vd, lam)
    yd = rms_norm(yd, diff_subln) * (1.0 - lambda_init)
    yd = yd.reshape(B, S, DIFF_V_WIDTH) * jax.nn.silu(zd)
    pb = yd @ w_proj_b

    ga, gb = jnp.split(jax.nn.sigmoid(gm), 2, axis=-1)
    y = (ga * pa + gb * pb) @ w_out
    y = rms_norm(y, norm_post)
    return x + gate[:, None] * y


def setup_inputs(seed: int = 0) -> dict:
    key = jax.random.key(seed)
    ks = jax.random.split(key, 20)
    D = D_MODEL
    nrm = jax.random.normal
    x = nrm(ks[0], (BATCH, SEQ, D), jnp.float32)
    c = nrm(ks[1], (BATCH, D), jnp.float32)
    positions = (jnp.arange(SEQ, dtype=jnp.int32)[None, :]
                 + jax.random.randint(ks[2], (BATCH, 1), 0, 1024, dtype=jnp.int32))
    w_ada = nrm(ks[3], (DEPTH, D, 3 * D), jnp.float32) * (0.5 * D ** -0.5)
    b_ada = nrm(ks[4], (DEPTH, 3 * D), jnp.float32) * 0.02
    norm_pre = 1.0 + 0.01 * nrm(ks[5], (DEPTH, D), jnp.float32)
    w_in = nrm(ks[6], (DEPTH, D, IN_WIDTH), jnp.float32) * D ** -0.5
    lambda_q1 = nrm(ks[7], (DEPTH, HEAD_DIM), jnp.float32) * 0.1
    lambda_k1 = nrm(ks[8], (DEPTH, HEAD_DIM), jnp.float32) * 0.1
    lambda_q2 = nrm(ks[9], (DEPTH, HEAD_DIM), jnp.float32) * 0.1
    lambda_k2 = nrm(ks[10], (DEPTH, HEAD_DIM), jnp.float32) * 0.1
    diff_subln = 1.0 + 0.01 * nrm(ks[11], (DEPTH, 2 * HEAD_DIM), jnp.float32)
    w_proj_a = nrm(ks[12], (DEPTH, DSW_OUT_WIDTH, D), jnp.float32) * DSW_OUT_WIDTH ** -0.5
    w_proj_b = nrm(ks[13], (DEPTH, DIFF_V_WIDTH, D), jnp.float32) * DIFF_V_WIDTH ** -0.5
    w_out = nrm(ks[14], (DEPTH, D, D), jnp.float32) * D ** -0.5
    norm_post = 1.0 + 0.01 * nrm(ks[15], (DEPTH, D), jnp.float32)
    return {"x": x, "c": c, "positions": positions, "w_ada": w_ada, "b_ada": b_ada,
            "norm_pre": norm_pre, "w_in": w_in, "lambda_q1": lambda_q1, "lambda_k1": lambda_k1,
            "lambda_q2": lambda_q2, "lambda_k2": lambda_k2, "diff_subln": diff_subln,
            "w_proj_a": w_proj_a, "w_proj_b": w_proj_b, "w_out": w_out, "norm_post": norm_post}


def reference(x, c, positions, w_ada, b_ada, norm_pre, w_in, lambda_q1, lambda_k1,
              lambda_q2, lambda_k2, diff_subln, w_proj_a, w_proj_b, w_out, norm_post):
    for l in range(DEPTH):
        lambda_init = 0.8 - 0.6 * math.exp(-0.3 * l)
        x = hybrid_layer(x, c, positions, w_ada[l], b_ada[l], norm_pre[l], w_in[l],
                         lambda_q1[l], lambda_k1[l], lambda_q2[l], lambda_k2[l], diff_subln[l],
                         w_proj_a[l], w_proj_b[l], w_out[l], norm_post[l], lambda_init)
    return x
```

```python
import functools
import math

import jax
import jax.numpy as jnp
from jax import lax
from jax.experimental import pallas as pl
from jax.experimental.pallas import tpu as pltpu

F32 = jnp.float32
BF16 = jnp.bfloat16

D_MODEL = 1024
BATCH = 16
SEQ = 2048
HEAD_DIM = 64
ROPE_THETA = 10000.0
EPS = 1e-6
NEG_INF = -1e30
DSW_DILATIONS = (1, 4, 16)
DSW_RADIUS = 64
DSW_HEADS_PER_GROUP = 8
DIFF_HEADS = 8
LAMBDA_INIT = 0.8 - 0.6 * math.exp(-0.3 * 0)

LANES = 128
IN_TILE_N = 512
QA_T, KA_T, VA_T, ZA_T, QD_T, KD_T, VD_T, ZD_T, GM_T, END_T = 0, 3, 6, 9, 10, 12, 14, 16, 18, 22
IN_WIDTH = END_T * IN_TILE_N
QA_B, KA_B, VA_B, ZA_B, QD_B, KD_B, VD_B, ZD_B = 0, 12, 24, 36, 40, 48, 56, 64
DSW_CHUNK = 128
DIFF_TQ = 256
OUT_TM = 1024
VMEM_LIMIT = 56 * 1024 * 1024


def _sigmoid(z):
    return 1.0 / (1.0 + jnp.exp(-z))


def _adaln_kernel(c_ref, w_ref, b_ref, o_ref):
    c = c_ref[...]
    sc = c * _sigmoid(c)
    o_ref[...] = jnp.dot(sc, w_ref[...], preferred_element_type=F32,
                         precision=lax.Precision.HIGHEST) + b_ref[...]


def _adaln(c, w_ada, b_ada):
    n_tiles = 3
    return pl.pallas_call(
        _adaln_kernel,
        grid=(n_tiles,),
        in_specs=[pl.BlockSpec((BATCH, D_MODEL), lambda n: (0, 0)),
                  pl.BlockSpec((D_MODEL, D_MODEL), lambda n: (0, n)),
                  pl.BlockSpec((1, D_MODEL), lambda n: (0, n))],
        out_specs=pl.BlockSpec((BATCH, D_MODEL), lambda n: (0, n)),
        out_shape=jax.ShapeDtypeStruct((BATCH, 3 * D_MODEL), F32),
        compiler_params=pltpu.CompilerParams(dimension_semantics=("arbitrary",),
                                             vmem_limit_bytes=VMEM_LIMIT),
        name="adaln",
    )(c, w_ada, b_ada)


def _inproj_kernel(x_ref, pos_ref, shift_ref, scale_ref, npre_ref, invf_ref, w_ref, o_ref,
                   h_scr, cos_scr, sin_scr):
    n = pl.program_id(1)

    @pl.when(n == 0)
    def _():
        rows = 256

        def body(i, carry):
            r = pl.multiple_of(i * rows, rows)
            x = x_ref[pl.ds(r, rows), :]
            ms = jnp.mean(x * x, axis=-1, keepdims=True)
            y = x * lax.rsqrt(ms + EPS) * npre_ref[...]
            h = y * (1.0 + scale_ref[...]) + shift_ref[...]
            h_scr[pl.ds(r, rows), :] = h.astype(BF16)
            ang = pos_ref[pl.ds(r, rows), :].astype(F32) * invf_ref[...]
            lane = lax.broadcasted_iota(jnp.int32, (1, LANES), 1)
            sgn = jnp.where((lane % HEAD_DIM) < HEAD_DIM // 2, -1.0, 1.0)
            cos_scr[pl.ds(r, rows), :] = jnp.cos(ang)
            sin_scr[pl.ds(r, rows), :] = jnp.sin(ang) * sgn
            return carry

        lax.fori_loop(0, SEQ // rows, body, 0)

    acc = jnp.dot(h_scr[...], w_ref[...], preferred_element_type=F32)
    is_q = (n < KA_T) | ((n >= QD_T) & (n < KD_T))
    is_rope = (n < VA_T) | ((n >= QD_T) & (n < VD_T))

    @pl.when(is_rope)
    def _():
        qscale = jnp.where(is_q, HEAD_DIM ** -0.5, 1.0).astype(F32)
        lane = lax.broadcasted_iota(jnp.int32, (1, LANES), 1)
        first_half = (lane % HEAD_DIM) < HEAD_DIM // 2
        cos = cos_scr[...] * qscale
        sin = sin_scr[...] * qscale
        for j in range(IN_TILE_N // LANES):
            a = acc[:, j * LANES:(j + 1) * LANES]
            swapped = jnp.where(first_half, pltpu.roll(a, LANES - HEAD_DIM // 2, 1),
                                pltpu.roll(a, HEAD_DIM // 2, 1))
            o_ref[:, j * LANES:(j + 1) * LANES] = (a * cos + swapped * sin).astype(BF16)

    @pl.when(jnp.logical_not(is_rope))
    def _():
        o_ref[...] = acc.astype(BF16)


def _inproj(x2d, pos2d, mod4, norm_pre, inv_freq, w_in_bf16):
    return pl.pallas_call(
        _inproj_kernel,
        grid=(BATCH, END_T),
        in_specs=[pl.BlockSpec((SEQ, D_MODEL), lambda b, n: (b, 0)),
                  pl.BlockSpec((SEQ, 1), lambda b, n: (b, 0)),
                  pl.BlockSpec((None, None, 1, D_MODEL), lambda b, n: (b, 0, 0, 0)),
                  pl.BlockSpec((None, None, 1, D_MODEL), lambda b, n: (b, 1, 0, 0)),
                  pl.BlockSpec((1, D_MODEL), lambda b, n: (0, 0)),
                  pl.BlockSpec((1, LANES), lambda b, n: (0, 0)),
                  pl.BlockSpec((D_MODEL, IN_TILE_N), lambda b, n: (0, n))],
        out_specs=pl.BlockSpec((SEQ, IN_TILE_N), lambda b, n: (b, n)),
        out_shape=jax.ShapeDtypeStruct((BATCH * SEQ, IN_WIDTH), BF16),
        scratch_shapes=[pltpu.VMEM((SEQ, D_MODEL), BF16),
                        pltpu.VMEM((SEQ, LANES), F32),
                        pltpu.VMEM((SEQ, LANES), F32)],
        compiler_params=pltpu.CompilerParams(dimension_semantics=("arbitrary", "arbitrary"),
                                             vmem_limit_bytes=VMEM_LIMIT),
        name="inproj",
    )(x2d, pos2d, mod4, mod4, norm_pre, inv_freq, w_in_bf16)


def _dsw_kernel(q0_ref, k0_ref, v0_ref, q1_ref, k1_ref, v1_ref, q2_ref, k2_ref, v2_ref,
                za_ref, o_ref, qf_scr, kf_scr, vf_scr, num_scr, max_scr, den_scr):
    lane = lax.broadcasted_iota(jnp.int32, (1, LANES), 1)
    lo = lane < HEAD_DIM
    groups = ((q0_ref, k0_ref, v0_ref), (q1_ref, k1_ref, v1_ref), (q2_ref, k2_ref, v2_ref))
    nt = (((1,), (1,)), ((), ()))

    for g, (q_ref, k_ref, v_ref) in enumerate(groups):
        dil = DSW_DILATIONS[g]
        sub_len = SEQ // dil
        win = min(2 * DSW_CHUNK, sub_len)
        if dil > 1:
            qf_scr[...] = q_ref[...].astype(F32)
            kf_scr[...] = k_ref[...].astype(F32)
            vf_scr[...] = v_ref[...].astype(F32)
        for res in range(dil):
            for c in range(sub_len // DSW_CHUNK):
                q_start = c * DSW_CHUNK
                k_start = min(max(q_start - DSW_RADIUS, 0), sub_len - win)
                if dil == 1:
                    q = q_ref[q_start:q_start + DSW_CHUNK, :]
                    kw = k_ref[k_start:k_start + win, :]
                    vw = v_ref[k_start:k_start + win, :]
                    rows = pl.ds(q_start, DSW_CHUNK)
                else:
                    rows = pl.ds(res + dil * q_start, DSW_CHUNK, stride=dil)
                    krows = pl.ds(res + dil * k_start, win, stride=dil)
                    q = qf_scr[rows, :].astype(BF16)
                    kw = kf_scr[krows, :].astype(BF16)
                    vw = vf_scr[krows, :].astype(BF16)
                zero = jnp.zeros_like(kw)
                s_a = lax.dot_general(q, jnp.where(lo, kw, zero), nt, preferred_element_type=F32)
                s_b = lax.dot_general(q, jnp.where(lo, zero, kw), nt, preferred_element_type=F32)
                qi = lax.broadcasted_iota(jnp.int32, (DSW_CHUNK, win), 0) + q_start
                kj = lax.broadcasted_iota(jnp.int32, (DSW_CHUNK, win), 1) + k_start
                band = jnp.abs(kj - qi) <= DSW_RADIUS
                s_a = jnp.where(band, s_a, NEG_INF)
                s_b = jnp.where(band, s_b, NEG_INF)
                m_a = jnp.max(s_a, axis=-1, keepdims=True)
                m_b = jnp.max(s_b, axis=-1, keepdims=True)
                e_a = jnp.exp(s_a - m_a)
                e_b = jnp.exp(s_b - m_b)
                d_a = jnp.sum(e_a, axis=-1, keepdims=True)
                d_b = jnp.sum(e_b, axis=-1, keepdims=True)
                num = (jnp.dot(e_a.astype(BF16), jnp.where(lo, vw, zero), preferred_element_type=F32)
                       + jnp.dot(e_b.astype(BF16), jnp.where(lo, zero, vw), preferred_element_type=F32))
                num_scr[g, rows, :] = num
                max_scr[g, rows, :] = jnp.where(lo, m_a, m_b)
                den_scr[g, rows, :] = jnp.where(lo, d_a, d_b)

    rows = 256

    def combine(i, carry):
        r = pl.ds(pl.multiple_of(i * rows, rows), rows)
        m0, m1, m2 = max_scr[0, r, :], max_scr[1, r, :], max_scr[2, r, :]
        m = jnp.maximum(jnp.maximum(m0, m1), m2)
        w0, w1, w2 = jnp.exp(m0 - m), jnp.exp(m1 - m), jnp.exp(m2 - m)
        num = w0 * num_scr[0, r, :] + w1 * num_scr[1, r, :] + w2 * num_scr[2, r, :]
        den = w0 * den_scr[0, r, :] + w1 * den_scr[1, r, :] + w2 * den_scr[2, r, :]
        z = za_ref[r, :].astype(F32)
        o_ref[r, :] = (num / den * (z * _sigmoid(z))).astype(BF16)
        return carry

    lax.fori_loop(0, SEQ // rows, combine, 0)


def _dsw_attention(proj):
    def col(base, g):
        return pl.BlockSpec((SEQ, LANES), lambda b, hp: (b, base + 4 * g + hp))

    in_specs = []
    for g in range(3):
        in_specs += [col(QA_B, g), col(KA_B, g), col(VA_B, g)]
    in_specs.append(pl.BlockSpec((SEQ, LANES), lambda b, hp: (b, ZA_B + hp)))
    return pl.pallas_call(
        _dsw_kernel,
        grid=(BATCH, DSW_HEADS_PER_GROUP // 2),
        in_specs=in_specs,
        out_specs=pl.BlockSpec((SEQ, LANES), lambda b, hp: (b, hp)),
        out_shape=jax.ShapeDtypeStruct((BATCH * SEQ, DSW_HEADS_PER_GROUP * HEAD_DIM), BF16),
        scratch_shapes=[pltpu.VMEM((SEQ, LANES), F32)] * 3 + [pltpu.VMEM((3, SEQ, LANES), F32)] * 3,
        compiler_params=pltpu.CompilerParams(dimension_semantics=("arbitrary", "arbitrary"),
                                             vmem_limit_bytes=VMEM_LIMIT),
        name="dsw_attn",
    )(*([proj] * 10))


def _diff_kernel(q_ref, k_ref, v_ref, z_ref, lq1_ref, lk1_ref, lq2_ref, lk2_ref, subln_ref, o_ref,
                 klo_scr, khi_scr):
    lane = lax.broadcasted_iota(jnp.int32, (1, LANES), 1)
    lo = lane < HEAD_DIM
    k = k_ref[...]
    zero = jnp.zeros_like(k)
    klo_scr[...] = jnp.where(lo, k, zero)
    khi_scr[...] = jnp.where(lo, zero, k)
    lam = (jnp.exp(jnp.sum(lq1_ref[...] * lk1_ref[...], axis=-1, keepdims=True))
           - jnp.exp(jnp.sum(lq2_ref[...] * lk2_ref[...], axis=-1, keepdims=True)) + LAMBDA_INIT)
    nt = (((1,), (1,)), ((), ()))

    def body(i, carry):
        r = pl.ds(pl.multiple_of(i * DIFF_TQ, DIFF_TQ), DIFF_TQ)
        q = q_ref[r, :]
        s_a = lax.dot_general(q, klo_scr[...], nt, preferred_element_type=F32)
        s_b = lax.dot_general(q, khi_scr[...], nt, preferred_element_type=F32)
        e_a = jnp.exp(s_a - jnp.max(s_a, axis=-1, keepdims=True))
        e_b = jnp.exp(s_b - jnp.max(s_b, axis=-1, keepdims=True))
        inv_a = 1.0 / jnp.sum(e_a, axis=-1, keepdims=True)
        inv_b = lam / jnp.sum(e_b, axis=-1, keepdims=True)
        a = e_a * inv_a - e_b * inv_b
        o = jnp.dot(a.astype(BF16), v_ref[...], preferred_element_type=F32)
        y = o * lax.rsqrt(jnp.mean(o * o, axis=-1, keepdims=True) + EPS) * subln_ref[...]
        y = y * (1.0 - LAMBDA_INIT)
        z = z_ref[r, :].astype(F32)
        o_ref[r, :] = (y * (z * _sigmoid(z))).astype(BF16)
        return carry

    lax.fori_loop(0, SEQ // DIFF_TQ, body, 0)


def _diff_attention(proj, lq1, lk1, lq2, lk2, subln):
    def col(base):
        return pl.BlockSpec((SEQ, LANES), lambda b, h: (b, base + h))

    small = pl.BlockSpec((1, HEAD_DIM), lambda b, h: (0, 0))
    return pl.pallas_call(
        _diff_kernel,
        grid=(BATCH, DIFF_HEADS),
        in_specs=[col(QD_B), col(KD_B), col(VD_B), col(ZD_B), small, small, small, small,
                  pl.BlockSpec((1, LANES), lambda b, h: (0, 0))],
        out_specs=pl.BlockSpec((SEQ, LANES), lambda b, h: (b, h)),
        out_shape=jax.ShapeDtypeStruct((BATCH * SEQ, DIFF_HEADS * LANES), BF16),
        scratch_shapes=[pltpu.VMEM((SEQ, LANES), BF16)] * 2,
        compiler_params=pltpu.CompilerParams(dimension_semantics=("arbitrary", "arbitrary"),
                                             vmem_limit_bytes=VMEM_LIMIT),
        name="diff_attn",
    )(proj, proj, proj, proj, lq1, lk1, lq2, lk2, subln)


def _out_kernel(ya_ref, yd_ref, ga_ref, gb_ref, x_ref, gate_ref, wpa_ref, wpb_ref, wout_ref, npost_ref,
                o_ref):
    pa = jnp.dot(ya_ref[...], wpa_ref[...], preferred_element_type=F32)
    pb = jnp.dot(yd_ref[...], wpb_ref[...], preferred_element_type=F32)
    mix = _sigmoid(ga_ref[...].astype(F32)) * pa + _sigmoid(gb_ref[...].astype(F32)) * pb
    y = jnp.dot(mix.astype(BF16), wout_ref[...], preferred_element_type=F32)
    y = y * lax.rsqrt(jnp.mean(y * y, axis=-1, keepdims=True) + EPS) * npost_ref[...]
    o_ref[...] = x_ref[...] + gate_ref[...] * y


def _out_stage(ya, yd, proj, x2d, mod4, wpa, wpb, wout, norm_post):
    per_batch = SEQ // OUT_TM
    gm_block = GM_T * IN_TILE_N // D_MODEL
    const = lambda i: (0, 0)
    return pl.pallas_call(
        _out_kernel,
        grid=(BATCH * per_batch,),
        in_specs=[pl.BlockSpec((OUT_TM, DSW_HEADS_PER_GROUP * HEAD_DIM), lambda i: (i, 0)),
                  pl.BlockSpec((OUT_TM, D_MODEL), lambda i: (i, 0)),
                  pl.BlockSpec((OUT_TM, D_MODEL), lambda i: (i, gm_block)),
                  pl.BlockSpec((OUT_TM, D_MODEL), lambda i: (i, gm_block + 1)),
                  pl.BlockSpec((OUT_TM, D_MODEL), lambda i: (i, 0)),
                  pl.BlockSpec((None, None, 1, D_MODEL), lambda i: (i // per_batch, 2, 0, 0)),
                  pl.BlockSpec((DSW_HEADS_PER_GROUP * HEAD_DIM, D_MODEL), const),
                  pl.BlockSpec((D_MODEL, D_MODEL), const),
                  pl.BlockSpec((D_MODEL, D_MODEL), const),
                  pl.BlockSpec((1, D_MODEL), const)],
        out_specs=pl.BlockSpec((OUT_TM, D_MODEL), lambda i: (i, 0)),
        out_shape=jax.ShapeDtypeStruct((BATCH * SEQ, D_MODEL), F32),
        compiler_params=pltpu.CompilerParams(dimension_semantics=("arbitrary",),
                                             vmem_limit_bytes=VMEM_LIMIT),
        name="out_stage",
    )(ya, yd, proj, proj, x2d, mod4, wpa, wpb, wout, norm_post)


def kernel(x, c, positions, w_ada, b_ada, norm_pre, w_in, lambda_q1, lambda_k1, lambda_q2, lambda_k2,
           diff_subln, w_proj_a, w_proj_b, w_out, norm_post):
    assert x.shape == (BATCH, SEQ, D_MODEL) and w_in.shape == (1, D_MODEL, IN_WIDTH)
    x2d = x.reshape(BATCH * SEQ, D_MODEL)
    pos2d = positions.reshape(BATCH * SEQ, 1)
    half = HEAD_DIM // 2
    inv_freq = ROPE_THETA ** (-jnp.arange(half, dtype=F32) / half)
    inv_freq = jnp.tile(inv_freq, LANES // half).reshape(1, LANES)

    mod = _adaln(c, w_ada[0], b_ada)
    mod4 = mod.reshape(BATCH, 3, 1, D_MODEL)
    proj = _inproj(x2d, pos2d, mod4, norm_pre, inv_freq, w_in[0].astype(BF16))
    ya = _dsw_attention(proj)
    yd = _diff_attention(proj, lambda_q1, lambda_k1, lambda_q2, lambda_k2, diff_subln)
    out = _out_stage(ya, yd, proj, x2d, mod4, w_proj_a[0].astype(BF16), w_proj_b[0].astype(BF16),
                     w_out[0].astype(BF16), norm_post)
    return out.reshape(BATCH, SEQ, D_MODEL)
```

```python
import functools
import math

import jax
import jax.numpy as jnp
from jax import lax
from jax.experimental import pallas as pl
from jax.experimental.pallas import tpu as pltpu

F32 = jnp.float32
BF16 = jnp.bfloat16

D_MODEL = 1024
BATCH = 16
SEQ = 2048
HEAD_DIM = 64
ROPE_THETA = 10000.0
EPS = 1e-6
NEG_INF = -1e30
DSW_DILATIONS = (1, 4, 16)
DSW_RADIUS = 64
DSW_HEADS_PER_GROUP = 8
DIFF_HEADS = 8
LAMBDA_INIT = 0.8 - 0.6 * math.exp(-0.3 * 0)

LANES = 128
IN_TILE_N = 512
IN_CHUNK_M = 256
Q_SCALE = HEAD_DIM ** -0.5 * math.log2(math.e)
QA_T, KA_T, VA_T, ZA_T, QD_T, KD_T, VD_T, ZD_T, GM_T, END_T = 0, 3, 6, 9, 10, 12, 14, 16, 18, 22
IN_WIDTH = END_T * IN_TILE_N
QA_B, KA_B, VA_B, ZA_B, QD_B, KD_B, VD_B, ZD_B = 0, 12, 24, 36, 40, 48, 56, 64
DSW_CHUNK = 128
DIFF_TQ = 256
OUT_TM = 1024
VMEM_LIMIT = 56 * 1024 * 1024


def _sigmoid(z):
    return 1.0 / (1.0 + jnp.exp(-z))


def _adaln_kernel(c_ref, w_ref, b_ref, o_ref):
    c = c_ref[...]
    sc = c * _sigmoid(c)
    o_ref[...] = jnp.dot(sc, w_ref[...], preferred_element_type=F32,
                         precision=lax.Precision.HIGHEST) + b_ref[...]


def _adaln(c, w_ada, b_ada):
    n_tiles = 3
    return pl.pallas_call(
        _adaln_kernel,
        grid=(n_tiles,),
        in_specs=[pl.BlockSpec((BATCH, D_MODEL), lambda n: (0, 0)),
                  pl.BlockSpec((D_MODEL, D_MODEL), lambda n: (0, n)),
                  pl.BlockSpec((1, D_MODEL), lambda n: (0, n))],
        out_specs=pl.BlockSpec((BATCH, D_MODEL), lambda n: (0, n)),
        out_shape=jax.ShapeDtypeStruct((BATCH, 3 * D_MODEL), F32),
        compiler_params=pltpu.CompilerParams(dimension_semantics=("arbitrary",),
                                             vmem_limit_bytes=VMEM_LIMIT),
        name="adaln",
    )(c, w_ada, b_ada)


def _inproj_kernel(x_ref, pos_ref, shift_ref, scale_ref, npre_ref, invf_ref, w_ref, o_ref,
                   h_scr, cos_scr, sin_scr):
    n = pl.program_id(1)

    @pl.when(n == 0)
    def _():
        rows = 256

        def body(i, carry):
            r = pl.multiple_of(i * rows, rows)
            x = x_ref[pl.ds(r, rows), :]
            ms = jnp.mean(x * x, axis=-1, keepdims=True)
            y = x * lax.rsqrt(ms + EPS) * npre_ref[...]
            h = y * (1.0 + scale_ref[...]) + shift_ref[...]
            h_scr[pl.ds(r, rows), :] = h.astype(BF16)
            ang = pos_ref[pl.ds(r, rows), :].astype(F32) * invf_ref[...]
            lane = lax.broadcasted_iota(jnp.int32, (1, LANES), 1)
            sgn = jnp.where(lane < LANES // 2, -1.0, 1.0)
            cos_scr[pl.ds(r, rows), :] = jnp.cos(ang)
            sin_scr[pl.ds(r, rows), :] = jnp.sin(ang) * sgn
            return carry

        lax.fori_loop(0, SEQ // rows, body, 0)

    is_q = (n < KA_T) | ((n >= QD_T) & (n < KD_T))
    is_rope = (n < VA_T) | ((n >= QD_T) & (n < VD_T))

    def chunks(epilogue):
        for c in range(SEQ // IN_CHUNK_M):
            rows = slice(c * IN_CHUNK_M, (c + 1) * IN_CHUNK_M)
            acc = jnp.dot(h_scr[rows, :], w_ref[...], preferred_element_type=F32)
            epilogue(rows, acc)

    @pl.when(is_rope)
    def _():
        qscale = jnp.where(is_q, Q_SCALE, 1.0).astype(F32)

        def rope(rows, acc):
            cos = cos_scr[rows, :] * qscale
            sin = sin_scr[rows, :] * qscale
            for j in range(IN_TILE_N // LANES):
                a = acc[:, j * LANES:(j + 1) * LANES]
                o_ref[rows, j * LANES:(j + 1) * LANES] = (
                    a * cos + pltpu.roll(a, LANES // 2, 1) * sin).astype(BF16)

        chunks(rope)

    @pl.when(jnp.logical_not(is_rope))
    def _():
        def plain(rows, acc):
            o_ref[rows, :] = acc.astype(BF16)

        chunks(plain)


def _inproj(x2d, pos2d, mod4, norm_pre, inv_freq, w_in_bf16):
    return pl.pallas_call(
        _inproj_kernel,
        grid=(BATCH, END_T),
        in_specs=[pl.BlockSpec((SEQ, D_MODEL), lambda b, n: (b, 0)),
                  pl.BlockSpec((SEQ, 1), lambda b, n: (b, 0)),
                  pl.BlockSpec((None, None, 1, D_MODEL), lambda b, n: (b, 0, 0, 0)),
                  pl.BlockSpec((None, None, 1, D_MODEL), lambda b, n: (b, 1, 0, 0)),
                  pl.BlockSpec((1, D_MODEL), lambda b, n: (0, 0)),
                  pl.BlockSpec((1, LANES), lambda b, n: (0, 0)),
                  pl.BlockSpec((D_MODEL, IN_TILE_N), lambda b, n: (0, n))],
        out_specs=pl.BlockSpec((SEQ, IN_TILE_N), lambda b, n: (b, n)),
        out_shape=jax.ShapeDtypeStruct((BATCH * SEQ, IN_WIDTH), BF16),
        scratch_shapes=[pltpu.VMEM((SEQ, D_MODEL), BF16),
                        pltpu.VMEM((SEQ, LANES), F32),
                        pltpu.VMEM((SEQ, LANES), F32)],
        compiler_params=pltpu.CompilerParams(dimension_semantics=("arbitrary", "arbitrary"),
                                             vmem_limit_bytes=VMEM_LIMIT),
        name="inproj",
    )(x2d, pos2d, mod4, mod4, norm_pre, inv_freq, w_in_bf16)


def _dsw_kernel(q0_ref, k0_ref, v0_ref, q1_ref, k1_ref, v1_ref, q2_ref, k2_ref, v2_ref,
                za_ref, o_ref, qf_scr, kf_scr, vf_scr, num_scr, max_scr, den_scr):
    lane = lax.broadcasted_iota(jnp.int32, (1, LANES), 1)
    lo = lane < HEAD_DIM
    qk_lo = (lane % HEAD_DIM) < HEAD_DIM // 2
    groups = ((q0_ref, k0_ref, v0_ref), (q1_ref, k1_ref, v1_ref), (q2_ref, k2_ref, v2_ref))
    nt = (((1,), (1,)), ((), ()))

    for g, (q_ref, k_ref, v_ref) in enumerate(groups):
        dil = DSW_DILATIONS[g]
        sub_len = SEQ // dil
        win = min(2 * DSW_CHUNK, sub_len)
        if dil > 1:
            qf_scr[...] = q_ref[...].astype(F32)
            kf_scr[...] = k_ref[...].astype(F32)
            vf_scr[...] = v_ref[...].astype(F32)
        for res in range(dil):
            for c in range(sub_len // DSW_CHUNK):
                q_start = c * DSW_CHUNK
                k_start = min(max(q_start - DSW_RADIUS, 0), sub_len - win)
                if dil == 1:
                    q = q_ref[q_start:q_start + DSW_CHUNK, :]
                    kw = k_ref[k_start:k_start + win, :]
                    vw = v_ref[k_start:k_start + win, :]
                    rows = pl.ds(q_start, DSW_CHUNK)
                else:
                    rows = pl.ds(res + dil * q_start, DSW_CHUNK, stride=dil)
                    krows = pl.ds(res + dil * k_start, win, stride=dil)
                    q = qf_scr[rows, :].astype(BF16)
                    kw = kf_scr[krows, :].astype(BF16)
                    vw = vf_scr[krows, :].astype(BF16)
                zero = jnp.zeros_like(kw)
                s_a = lax.dot_general(q, jnp.where(qk_lo, kw, zero), nt, preferred_element_type=F32)
                s_b = lax.dot_general(q, jnp.where(qk_lo, zero, kw), nt, preferred_element_type=F32)
                qi = lax.broadcasted_iota(jnp.int32, (DSW_CHUNK, win), 0) + q_start
                kj = lax.broadcasted_iota(jnp.int32, (DSW_CHUNK, win), 1) + k_start
                band = jnp.abs(kj - qi) <= DSW_RADIUS
                s_a = jnp.where(band, s_a, NEG_INF)
                s_b = jnp.where(band, s_b, NEG_INF)
                m_a = jnp.max(s_a, axis=-1, keepdims=True)
                m_b = jnp.max(s_b, axis=-1, keepdims=True)
                e_a = jnp.exp2(s_a - m_a)
                e_b = jnp.exp2(s_b - m_b)
                d_a = jnp.sum(e_a, axis=-1, keepdims=True)
                d_b = jnp.sum(e_b, axis=-1, keepdims=True)
                num = (jnp.dot(e_a.astype(BF16), jnp.where(lo, vw, zero), preferred_element_type=F32)
                       + jnp.dot(e_b.astype(BF16), jnp.where(lo, zero, vw), preferred_element_type=F32))
                num_scr[g, rows, :] = num
                max_scr[g, rows, :] = jnp.where(lo, m_a, m_b)
                den_scr[g, rows, :] = jnp.where(lo, d_a, d_b)

    rows = 256

    def combine(i, carry):
        r = pl.ds(pl.multiple_of(i * rows, rows), rows)
        m0, m1, m2 = max_scr[0, r, :], max_scr[1, r, :], max_scr[2, r, :]
        m = jnp.maximum(jnp.maximum(m0, m1), m2)
        w0, w1, w2 = jnp.exp2(m0 - m), jnp.exp2(m1 - m), jnp.exp2(m2 - m)
        num = w0 * num_scr[0, r, :] + w1 * num_scr[1, r, :] + w2 * num_scr[2, r, :]
        den = w0 * den_scr[0, r, :] + w1 * den_scr[1, r, :] + w2 * den_scr[2, r, :]
        z = za_ref[r, :].astype(F32)
        o_ref[r, :] = (num / den * (z * _sigmoid(z))).astype(BF16)
        return carry

    lax.fori_loop(0, SEQ // rows, combine, 0)


def _dsw_attention(proj):
    def col(base, g):
        return pl.BlockSpec((SEQ, LANES), lambda b, hp: (b, base + 4 * g + hp))

    in_specs = []
    for g in range(3):
        in_specs += [col(QA_B, g), col(KA_B, g), col(VA_B, g)]
    in_specs.append(pl.BlockSpec((SEQ, LANES), lambda b, hp: (b, ZA_B + hp)))
    return pl.pallas_call(
        _dsw_kernel,
        grid=(BATCH, DSW_HEADS_PER_GROUP // 2),
        in_specs=in_specs,
        out_specs=pl.BlockSpec((SEQ, LANES), lambda b, hp: (b, hp)),
        out_shape=jax.ShapeDtypeStruct((BATCH * SEQ, DSW_HEADS_PER_GROUP * HEAD_DIM), BF16),
        scratch_shapes=[pltpu.VMEM((SEQ, LANES), F32)] * 3 + [pltpu.VMEM((3, SEQ, LANES), F32)] * 3,
        compiler_params=pltpu.CompilerParams(dimension_semantics=("arbitrary", "arbitrary"),
                                             vmem_limit_bytes=VMEM_LIMIT),
        name="dsw_attn",
    )(*([proj] * 10))


def _diff_kernel(q_ref, k_ref, v_ref, z_ref, lq1_ref, lk1_ref, lq2_ref, lk2_ref, subln_ref, o_ref,
                 klo_scr, khi_scr):
    lane = lax.broadcasted_iota(jnp.int32, (1, LANES), 1)
    lo = (lane % HEAD_DIM) < HEAD_DIM // 2
    k = k_ref[...]
    zero = jnp.zeros_like(k)
    klo_scr[...] = jnp.where(lo, k, zero)
    khi_scr[...] = jnp.where(lo, zero, k)
    lam = (jnp.exp(jnp.sum(lq1_ref[...] * lk1_ref[...], axis=-1, keepdims=True))
           - jnp.exp(jnp.sum(lq2_ref[...] * lk2_ref[...], axis=-1, keepdims=True)) + LAMBDA_INIT)
    nt = (((1,), (1,)), ((), ()))

    def body(i, carry):
        r = pl.ds(pl.multiple_of(i * DIFF_TQ, DIFF_TQ), DIFF_TQ)
        q = q_ref[r, :]
        s_a = lax.dot_general(q, klo_scr[...], nt, preferred_element_type=F32)
        s_b = lax.dot_general(q, khi_scr[...], nt, preferred_element_type=F32)
        e_a = jnp.exp2(s_a - jnp.max(s_a, axis=-1, keepdims=True))
        e_b = jnp.exp2(s_b - jnp.max(s_b, axis=-1, keepdims=True))
        inv_a = 1.0 / jnp.sum(e_a, axis=-1, keepdims=True)
        inv_b = lam / jnp.sum(e_b, axis=-1, keepdims=True)
        a = e_a * inv_a - e_b * inv_b
        o = jnp.dot(a.astype(BF16), v_ref[...], preferred_element_type=F32)
        y = o * lax.rsqrt(jnp.mean(o * o, axis=-1, keepdims=True) + EPS) * subln_ref[...]
        y = y * (1.0 - LAMBDA_INIT)
        z = z_ref[r, :].astype(F32)
        o_ref[r, :] = (y * (z * _sigmoid(z))).astype(BF16)
        return carry

    lax.fori_loop(0, SEQ // DIFF_TQ, body, 0, unroll=2)


def _diff_attention(proj, lq1, lk1, lq2, lk2, subln):
    def col(base):
        return pl.BlockSpec((SEQ, LANES), lambda b, h: (b, base + h))

    small = pl.BlockSpec((1, HEAD_DIM), lambda b, h: (0, 0))
    return pl.pallas_call(
        _diff_kernel,
        grid=(BATCH, DIFF_HEADS),
        in_specs=[col(QD_B), col(KD_B), col(VD_B), col(ZD_B), small, small, small, small,
                  pl.BlockSpec((1, LANES), lambda b, h: (0, 0))],
        out_specs=pl.BlockSpec((SEQ, LANES), lambda b, h: (b, h)),
        out_shape=jax.ShapeDtypeStruct((BATCH * SEQ, DIFF_HEADS * LANES), BF16),
        scratch_shapes=[pltpu.VMEM((SEQ, LANES), BF16)] * 2,
        compiler_params=pltpu.CompilerParams(dimension_semantics=("arbitrary", "arbitrary"),
                                             vmem_limit_bytes=VMEM_LIMIT),
        name="diff_attn",
    )(proj, proj, proj, proj, lq1, lk1, lq2, lk2, subln)


def _out_kernel(ya_ref, yd_ref, ga_ref, gb_ref, x_ref, gate_ref, wpa_ref, wpb_ref, wout_ref, npost_ref,
                o_ref):
    pa = jnp.dot(ya_ref[...], wpa_ref[...], preferred_element_type=F32)
    pb = jnp.dot(yd_ref[...], wpb_ref[...], preferred_element_type=F32)
    mix = _sigmoid(ga_ref[...].astype(F32)) * pa + _sigmoid(gb_ref[...].astype(F32)) * pb
    y = jnp.dot(mix.astype(BF16), wout_ref[...], preferred_element_type=F32)
    y = y * lax.rsqrt(jnp.mean(y * y, axis=-1, keepdims=True) + EPS) * npost_ref[...]
    o_ref[...] = x_ref[...] + gate_ref[...] * y


def _out_stage(ya, yd, proj, x2d, mod4, wpa, wpb, wout, norm_post):
    per_batch = SEQ // OUT_TM
    gm_block = GM_T * IN_TILE_N // D_MODEL
    const = lambda i: (0, 0)
    return pl.pallas_call(
        _out_kernel,
        grid=(BATCH * per_batch,),
        in_specs=[pl.BlockSpec((OUT_TM, DSW_HEADS_PER_GROUP * HEAD_DIM), lambda i: (i, 0)),
                  pl.BlockSpec((OUT_TM, D_MODEL), lambda i: (i, 0)),
                  pl.BlockSpec((OUT_TM, D_MODEL), lambda i: (i, gm_block)),
                  pl.BlockSpec((OUT_TM, D_MODEL), lambda i: (i, gm_block + 1)),
                  pl.BlockSpec((OUT_TM, D_MODEL), lambda i: (i, 0)),
                  pl.BlockSpec((None, None, 1, D_MODEL), lambda i: (i // per_batch, 2, 0, 0)),
                  pl.BlockSpec((DSW_HEADS_PER_GROUP * HEAD_DIM, D_MODEL), const),
                  pl.BlockSpec((D_MODEL, D_MODEL), const),
                  pl.BlockSpec((D_MODEL, D_MODEL), const),
                  pl.BlockSpec((1, D_MODEL), const)],
        out_specs=pl.BlockSpec((OUT_TM, D_MODEL), lambda i: (i, 0)),
        out_shape=jax.ShapeDtypeStruct((BATCH * SEQ, D_MODEL), F32),
        compiler_params=pltpu.CompilerParams(dimension_semantics=("arbitrary",),
                                             vmem_limit_bytes=VMEM_LIMIT),
        name="out_stage",
    )(ya, yd, proj, proj, x2d, mod4, wpa, wpb, wout, norm_post)


def _pair_rope_halves(w):
    d, n = w.shape
    w = w.reshape(d, n // LANES, 2, 2, HEAD_DIM // 2)
    return w.transpose(0, 1, 3, 2, 4).reshape(d, n)


def kernel(x, c, positions, w_ada, b_ada, norm_pre, w_in, lambda_q1, lambda_k1, lambda_q2, lambda_k2,
           diff_subln, w_proj_a, w_proj_b, w_out, norm_post):
    assert x.shape == (BATCH, SEQ, D_MODEL) and w_in.shape == (1, D_MODEL, IN_WIDTH)
    x2d = x.reshape(BATCH * SEQ, D_MODEL)
    pos2d = positions.reshape(BATCH * SEQ, 1)
    half = HEAD_DIM // 2
    inv_freq = ROPE_THETA ** (-jnp.arange(half, dtype=F32) / half)
    inv_freq = jnp.tile(inv_freq, LANES // half).reshape(1, LANES)

    mod = _adaln(c, w_ada[0], b_ada)
    mod4 = mod.reshape(BATCH, 3, 1, D_MODEL)
    w = w_in[0]
    q_end, v_end, kd_end = VA_T * IN_TILE_N, QD_T * IN_TILE_N, VD_T * IN_TILE_N
    w = jnp.concatenate([_pair_rope_halves(w[:, :q_end]), w[:, q_end:v_end],
                         _pair_rope_halves(w[:, v_end:kd_end]), w[:, kd_end:]], axis=1)
    proj = _inproj(x2d, pos2d, mod4, norm_pre, inv_freq, w.astype(BF16))
    ya = _dsw_attention(proj)
    yd = _diff_attention(proj, lambda_q1, lambda_k1, lambda_q2, lambda_k2, diff_subln)
    out = _out_stage(ya, yd, proj, x2d, mod4, w_proj_a[0].astype(BF16), w_proj_b[0].astype(BF16),
                     w_out[0].astype(BF16), norm_post)
    return out.reshape(BATCH, SEQ, D_MODEL)
```

```python
import functools
import math

import jax
import jax.numpy as jnp
from jax import lax
from jax.experimental import pallas as pl
from jax.experimental.pallas import tpu as pltpu

F32 = jnp.float32
BF16 = jnp.bfloat16

D_MODEL = 1024
BATCH = 16
SEQ = 2048
HEAD_DIM = 64
ROPE_THETA = 10000.0
EPS = 1e-6
NEG_INF = -1e30
DSW_DILATIONS = (1, 4, 16)
DSW_RADIUS = 64
DSW_HEADS_PER_GROUP = 8
DIFF_HEADS = 8
LAMBDA_INIT = 0.8 - 0.6 * math.exp(-0.3 * 0)

LANES = 128
IN_TILE_N = 512
IN_CHUNK_M = 256
Q_SCALE = HEAD_DIM ** -0.5 * math.log2(math.e)
QA_T, KA_T, VA_T, ZA_T, QD_T, KD_T, VD_T, ZD_T, GM_T, END_T = 0, 3, 6, 9, 10, 12, 14, 16, 18, 22
IN_WIDTH = END_T * IN_TILE_N
QA_B, KA_B, VA_B, ZA_B, QD_B, KD_B, VD_B, ZD_B = 0, 12, 24, 36, 40, 48, 56, 64
DSW_CHUNK = 128
DIFF_TQ = 256
DIFF_TK = 256
DIFF_SAFE_LOG2 = 100.0
OUT_TM = 1024
VMEM_LIMIT = 56 * 1024 * 1024


def _sigmoid(z):
    return 1.0 / (1.0 + jnp.exp(-z))


def _adaln_kernel(c_ref, w_ref, b_ref, o_ref):
    c = c_ref[...]
    sc = c * _sigmoid(c)
    o_ref[...] = jnp.dot(sc, w_ref[...], preferred_element_type=F32,
                         precision=lax.Precision.HIGHEST) + b_ref[...]


def _adaln(c, w_ada, b_ada):
    n_tiles = 3
    return pl.pallas_call(
        _adaln_kernel,
        grid=(n_tiles,),
        in_specs=[pl.BlockSpec((BATCH, D_MODEL), lambda n: (0, 0)),
                  pl.BlockSpec((D_MODEL, D_MODEL), lambda n: (0, n)),
                  pl.BlockSpec((1, D_MODEL), lambda n: (0, n))],
        out_specs=pl.BlockSpec((BATCH, D_MODEL), lambda n: (0, n)),
        out_shape=jax.ShapeDtypeStruct((BATCH, 3 * D_MODEL), F32),
        compiler_params=pltpu.CompilerParams(dimension_semantics=("arbitrary",),
                                             vmem_limit_bytes=VMEM_LIMIT),
        name="adaln",
    )(c, w_ada, b_ada)


def _inproj_kernel(x_ref, pos_ref, shift_ref, scale_ref, npre_ref, invf_ref, w_ref, o_ref,
                   h_scr, cos_scr, sin_scr):
    n = pl.program_id(1)

    @pl.when(n == 0)
    def _():
        rows = 256

        def body(i, carry):
            r = pl.multiple_of(i * rows, rows)
            x = x_ref[pl.ds(r, rows), :]
            ms = jnp.mean(x * x, axis=-1, keepdims=True)
            y = x * lax.rsqrt(ms + EPS) * npre_ref[...]
            h = y * (1.0 + scale_ref[...]) + shift_ref[...]
            h_scr[pl.ds(r, rows), :] = h.astype(BF16)
            ang = pos_ref[pl.ds(r, rows), :].astype(F32) * invf_ref[...]
            lane = lax.broadcasted_iota(jnp.int32, (1, LANES), 1)
            sgn = jnp.where(lane < LANES // 2, -1.0, 1.0)
            cos_scr[pl.ds(r, rows), :] = jnp.cos(ang)
            sin_scr[pl.ds(r, rows), :] = jnp.sin(ang) * sgn
            return carry

        lax.fori_loop(0, SEQ // rows, body, 0)

    is_q = (n < KA_T) | ((n >= QD_T) & (n < KD_T))
    is_rope = (n < VA_T) | ((n >= QD_T) & (n < VD_T))

    def chunks(epilogue):
        for c in range(SEQ // IN_CHUNK_M):
            rows = slice(c * IN_CHUNK_M, (c + 1) * IN_CHUNK_M)
            acc = jnp.dot(h_scr[rows, :], w_ref[...], preferred_element_type=F32)
            epilogue(rows, acc)

    @pl.when(is_rope)
    def _():
        qscale = jnp.where(is_q, Q_SCALE, 1.0).astype(F32)

        def rope(rows, acc):
            cos = cos_scr[rows, :] * qscale
            sin = sin_scr[rows, :] * qscale
            for j in range(IN_TILE_N // LANES):
                a = acc[:, j * LANES:(j + 1) * LANES]
                o_ref[rows, j * LANES:(j + 1) * LANES] = (
                    a * cos + pltpu.roll(a, LANES // 2, 1) * sin).astype(BF16)

        chunks(rope)

    @pl.when(jnp.logical_not(is_rope))
    def _():
        def plain(rows, acc):
            o_ref[rows, :] = acc.astype(BF16)

        chunks(plain)


def _inproj(x2d, pos2d, mod4, norm_pre, inv_freq, w_in_bf16):
    return pl.pallas_call(
        _inproj_kernel,
        grid=(BATCH, END_T),
        in_specs=[pl.BlockSpec((SEQ, D_MODEL), lambda b, n: (b, 0)),
                  pl.BlockSpec((SEQ, 1), lambda b, n: (b, 0)),
                  pl.BlockSpec((None, None, 1, D_MODEL), lambda b, n: (b, 0, 0, 0)),
                  pl.BlockSpec((None, None, 1, D_MODEL), lambda b, n: (b, 1, 0, 0)),
                  pl.BlockSpec((1, D_MODEL), lambda b, n: (0, 0)),
                  pl.BlockSpec((1, LANES), lambda b, n: (0, 0)),
                  pl.BlockSpec((D_MODEL, IN_TILE_N), lambda b, n: (0, n))],
        out_specs=pl.BlockSpec((SEQ, IN_TILE_N), lambda b, n: (b, n)),
        out_shape=jax.ShapeDtypeStruct((BATCH * SEQ, IN_WIDTH), BF16),
        scratch_shapes=[pltpu.VMEM((SEQ, D_MODEL), BF16),
                        pltpu.VMEM((SEQ, LANES), F32),
                        pltpu.VMEM((SEQ, LANES), F32)],
        compiler_params=pltpu.CompilerParams(dimension_semantics=("arbitrary", "arbitrary"),
                                             vmem_limit_bytes=VMEM_LIMIT),
        name="inproj",
    )(x2d, pos2d, mod4, mod4, norm_pre, inv_freq, w_in_bf16)


def _dsw_kernel(q0_ref, k0_ref, v0_ref, q1_ref, k1_ref, v1_ref, q2_ref, k2_ref, v2_ref,
                za_ref, o_ref, qf_scr, kf_scr, vf_scr, num_scr, max_scr, den_scr):
    lane = lax.broadcasted_iota(jnp.int32, (1, LANES), 1)
    lo = lane < HEAD_DIM
    qk_lo = (lane % HEAD_DIM) < HEAD_DIM // 2
    groups = ((q0_ref, k0_ref, v0_ref), (q1_ref, k1_ref, v1_ref), (q2_ref, k2_ref, v2_ref))
    nt = (((1,), (1,)), ((), ()))

    for g, (q_ref, k_ref, v_ref) in enumerate(groups):
        dil = DSW_DILATIONS[g]
        sub_len = SEQ // dil
        win = min(2 * DSW_CHUNK, sub_len)
        if dil > 1:
            qf_scr[...] = q_ref[...].astype(F32)
            kf_scr[...] = k_ref[...].astype(F32)
            vf_scr[...] = v_ref[...].astype(F32)
        for res in range(dil):
            for c in range(sub_len // DSW_CHUNK):
                q_start = c * DSW_CHUNK
                k_start = min(max(q_start - DSW_RADIUS, 0), sub_len - win)
                if dil == 1:
                    q = q_ref[q_start:q_start + DSW_CHUNK, :]
                    kw = k_ref[k_start:k_start + win, :]
                    vw = v_ref[k_start:k_start + win, :]
                    rows = pl.ds(q_start, DSW_CHUNK)
                else:
                    rows = pl.ds(res + dil * q_start, DSW_CHUNK, stride=dil)
                    krows = pl.ds(res + dil * k_start, win, stride=dil)
                    q = qf_scr[rows, :].astype(BF16)
                    kw = kf_scr[krows, :].astype(BF16)
                    vw = vf_scr[krows, :].astype(BF16)
                zero = jnp.zeros_like(kw)
                s_a = lax.dot_general(q, jnp.where(qk_lo, kw, zero), nt, preferred_element_type=F32)
                s_b = lax.dot_general(q, jnp.where(qk_lo, zero, kw), nt, preferred_element_type=F32)
                qi = lax.broadcasted_iota(jnp.int32, (DSW_CHUNK, win), 0) + q_start
                kj = lax.broadcasted_iota(jnp.int32, (DSW_CHUNK, win), 1) + k_start
                band = jnp.abs(kj - qi) <= DSW_RADIUS
                s_a = jnp.where(band, s_a, NEG_INF)
                s_b = jnp.where(band, s_b, NEG_INF)
                m_a = jnp.max(s_a, axis=-1, keepdims=True)
                m_b = jnp.max(s_b, axis=-1, keepdims=True)
                e_a = jnp.exp2(s_a - m_a)
                e_b = jnp.exp2(s_b - m_b)
                d_a = jnp.sum(e_a, axis=-1, keepdims=True)
                d_b = jnp.sum(e_b, axis=-1, keepdims=True)
                num = (jnp.dot(e_a.astype(BF16), jnp.where(lo, vw, zero), preferred_element_type=F32)
                       + jnp.dot(e_b.astype(BF16), jnp.where(lo, zero, vw), preferred_element_type=F32))
                num_scr[g, rows, :] = num
                max_scr[g, rows, :] = jnp.where(lo, m_a, m_b)
                den_scr[g, rows, :] = jnp.where(lo, d_a, d_b)

    rows = 256

    def combine(i, carry):
        r = pl.ds(pl.multiple_of(i * rows, rows), rows)
        m0, m1, m2 = max_scr[0, r, :], max_scr[1, r, :], max_scr[2, r, :]
        m = jnp.maximum(jnp.maximum(m0, m1), m2)
        w0, w1, w2 = jnp.exp2(m0 - m), jnp.exp2(m1 - m), jnp.exp2(m2 - m)
        num = w0 * num_scr[0, r, :] + w1 * num_scr[1, r, :] + w2 * num_scr[2, r, :]
        den = w0 * den_scr[0, r, :] + w1 * den_scr[1, r, :] + w2 * den_scr[2, r, :]
        z = za_ref[r, :].astype(F32)
        o_ref[r, :] = (num / den * (z * _sigmoid(z))).astype(BF16)
        return carry

    lax.fori_loop(0, SEQ // rows, combine, 0)


def _dsw_attention(proj):
    def col(base, g):
        return pl.BlockSpec((SEQ, LANES), lambda b, hp: (b, base + 4 * g + hp))

    in_specs = []
    for g in range(3):
        in_specs += [col(QA_B, g), col(KA_B, g), col(VA_B, g)]
    in_specs.append(pl.BlockSpec((SEQ, LANES), lambda b, hp: (b, ZA_B + hp)))
    return pl.pallas_call(
        _dsw_kernel,
        grid=(BATCH, DSW_HEADS_PER_GROUP // 2),
        in_specs=in_specs,
        out_specs=pl.BlockSpec((SEQ, LANES), lambda b, hp: (b, hp)),
        out_shape=jax.ShapeDtypeStruct((BATCH * SEQ, DSW_HEADS_PER_GROUP * HEAD_DIM), BF16),
        scratch_shapes=[pltpu.VMEM((SEQ, LANES), F32)] * 3 + [pltpu.VMEM((3, SEQ, LANES), F32)] * 3,
        compiler_params=pltpu.CompilerParams(dimension_semantics=("arbitrary", "arbitrary"),
                                             vmem_limit_bytes=VMEM_LIMIT),
        name="dsw_attn",
    )(*([proj] * 10))


def _max_row_norm2(ref):
    x = ref[...].astype(F32)
    sq = x * x
    paired = sq[:SEQ // 2, :] + sq[SEQ // 2:, :]
    return jnp.max(jnp.sum(paired, axis=-1, keepdims=True))


def _diff_kernel(q_ref, k_ref, v_ref, z_ref, lq1_ref, lk1_ref, lq2_ref, lk2_ref, subln_ref, o_ref,
                 klo_scr, khi_scr, e_scr, d_scr):
    lane = lax.broadcasted_iota(jnp.int32, (1, LANES), 1)
    lo = (lane % HEAD_DIM) < HEAD_DIM // 2
    k = k_ref[...]
    zero = jnp.zeros_like(k)
    klo_scr[...] = jnp.where(lo, k, zero)
    khi_scr[...] = jnp.where(lo, zero, k)
    lam = (jnp.exp(jnp.sum(lq1_ref[...] * lk1_ref[...], axis=-1, keepdims=True))
           - jnp.exp(jnp.sum(lq2_ref[...] * lk2_ref[...], axis=-1, keepdims=True)) + LAMBDA_INIT)
    nt = (((1,), (1,)), ((), ()))
    n_kt = SEQ // DIFF_TK
    n_chunks = SEQ // DIFF_TQ

    def rows_of(c):
        return pl.ds(pl.multiple_of(c * DIFF_TQ, DIFF_TQ), DIFF_TQ)

    def lane_partial_sum(e, acc):
        for j in range(e.shape[1] // LANES):
            part = e[:, j * LANES:(j + 1) * LANES]
            acc = part if acc is None else acc + part
        return acc

    def weights_unshifted(c, slot):
        q = q_ref[rows_of(c), :]
        for half, k_scr in enumerate((klo_scr, khi_scr)):
            den = None
            for kt in range(n_kt):
                ks = slice(kt * DIFF_TK, (kt + 1) * DIFF_TK)
                e = jnp.exp2(lax.dot_general(q, k_scr[ks, :], nt, preferred_element_type=F32))
                e_scr[slot, half, :, ks] = e.astype(BF16)
                den = lane_partial_sum(e, den)
            d_scr[slot, half] = den

    def weights_max_shifted(c, slot):
        q = q_ref[rows_of(c), :]
        for half, k_scr in enumerate((klo_scr, khi_scr)):
            s = lax.dot_general(q, k_scr[...], nt, preferred_element_type=F32)
            e = jnp.exp2(s - jnp.max(s, axis=-1, keepdims=True))
            e_scr[slot, half] = e.astype(BF16)
            d_scr[slot, half] = lane_partial_sum(e, None)

    def normalise_pv(c, slot):
        inv_a = (1.0 / jnp.sum(d_scr[slot, 0], axis=-1, keepdims=True)).astype(BF16)
        inv_b = (lam / jnp.sum(d_scr[slot, 1], axis=-1, keepdims=True)).astype(BF16)
        o = None
        for kt in range(n_kt):
            ks = slice(kt * DIFF_TK, (kt + 1) * DIFF_TK)
            a = e_scr[slot, 0, :, ks] * inv_a - e_scr[slot, 1, :, ks] * inv_b
            pv = jnp.dot(a, v_ref[ks, :], preferred_element_type=F32)
            o = pv if o is None else o + pv
        y = o * lax.rsqrt(jnp.mean(o * o, axis=-1, keepdims=True) + EPS) * subln_ref[...]
        y = y * (1.0 - LAMBDA_INIT)
        z = z_ref[rows_of(c), :].astype(F32)
        o_ref[rows_of(c), :] = (y * (z * _sigmoid(z))).astype(BF16)

    bounded = _max_row_norm2(q_ref) * _max_row_norm2(k_ref) <= DIFF_SAFE_LOG2 ** 2

    @pl.when(bounded)
    def _():
        weights_unshifted(0, 0)

        def body(j, carry):
            c0 = 2 * j
            weights_unshifted(c0 + 1, 1)
            normalise_pv(c0, 0)
            weights_unshifted(jnp.minimum(c0 + 2, n_chunks - 1), 0)
            normalise_pv(c0 + 1, 1)
            return carry

        lax.fori_loop(0, n_chunks // 2, body, 0)

    @pl.when(jnp.logical_not(bounded))
    def _():
        def body(c, carry):
            weights_max_shifted(c, 0)
            normalise_pv(c, 0)
            return carry

        lax.fori_loop(0, n_chunks, body, 0)


def _diff_attention(proj, lq1, lk1, lq2, lk2, subln):
    def col(base):
        return pl.BlockSpec((SEQ, LANES), lambda b, h: (b, base + h))

    small = pl.BlockSpec((1, HEAD_DIM), lambda b, h: (0, 0))
    return pl.pallas_call(
        _diff_kernel,
        grid=(BATCH, DIFF_HEADS),
        in_specs=[col(QD_B), col(KD_B), col(VD_B), col(ZD_B), small, small, small, small,
                  pl.BlockSpec((1, LANES), lambda b, h: (0, 0))],
        out_specs=pl.BlockSpec((SEQ, LANES), lambda b, h: (b, h)),
        out_shape=jax.ShapeDtypeStruct((BATCH * SEQ, DIFF_HEADS * LANES), BF16),
        scratch_shapes=[pltpu.VMEM((SEQ, LANES), BF16)] * 2
        + [pltpu.VMEM((2, 2, DIFF_TQ, SEQ), BF16),
           pltpu.VMEM((2, 2, DIFF_TQ, LANES), F32)],
        compiler_params=pltpu.CompilerParams(dimension_semantics=("arbitrary", "arbitrary"),
                                             vmem_limit_bytes=VMEM_LIMIT),
        name="diff_attn",
    )(proj, proj, proj, proj, lq1, lk1, lq2, lk2, subln)


def _out_kernel(ya_ref, yd_ref, ga_ref, gb_ref, x_ref, gate_ref, wpa_ref, wpb_ref, wout_ref, npost_ref,
                o_ref):
    pa = jnp.dot(ya_ref[...], wpa_ref[...], preferred_element_type=F32)
    pb = jnp.dot(yd_ref[...], wpb_ref[...], preferred_element_type=F32)
    mix = _sigmoid(ga_ref[...].astype(F32)) * pa + _sigmoid(gb_ref[...].astype(F32)) * pb
    y = jnp.dot(mix.astype(BF16), wout_ref[...], preferred_element_type=F32)
    y = y * lax.rsqrt(jnp.mean(y * y, axis=-1, keepdims=True) + EPS) * npost_ref[...]
    o_ref[...] = x_ref[...] + gate_ref[...] * y


def _out_stage(ya, yd, proj, x2d, mod4, wpa, wpb, wout, norm_post):
    per_batch = SEQ // OUT_TM
    gm_block = GM_T * IN_TILE_N // D_MODEL
    const = lambda i: (0, 0)
    return pl.pallas_call(
        _out_kernel,
        grid=(BATCH * per_batch,),
        in_specs=[pl.BlockSpec((OUT_TM, DSW_HEADS_PER_GROUP * HEAD_DIM), lambda i: (i, 0)),
                  pl.BlockSpec((OUT_TM, D_MODEL), lambda i: (i, 0)),
                  pl.BlockSpec((OUT_TM, D_MODEL), lambda i: (i, gm_block)),
                  pl.BlockSpec((OUT_TM, D_MODEL), lambda i: (i, gm_block + 1)),
                  pl.BlockSpec((OUT_TM, D_MODEL), lambda i: (i, 0)),
                  pl.BlockSpec((None, None, 1, D_MODEL), lambda i: (i // per_batch, 2, 0, 0)),
                  pl.BlockSpec((DSW_HEADS_PER_GROUP * HEAD_DIM, D_MODEL), const),
                  pl.BlockSpec((D_MODEL, D_MODEL), const),
                  pl.BlockSpec((D_MODEL, D_MODEL), const),
                  pl.BlockSpec((1, D_MODEL), const)],
        out_specs=pl.BlockSpec((OUT_TM, D_MODEL), lambda i: (i, 0)),
        out_shape=jax.ShapeDtypeStruct((BATCH * SEQ, D_MODEL), F32),
        compiler_params=pltpu.CompilerParams(dimension_semantics=("arbitrary",),
                                             vmem_limit_bytes=VMEM_LIMIT),
        name="out_stage",
    )(ya, yd, proj, proj, x2d, mod4, wpa, wpb, wout, norm_post)


def _pair_rope_halves(w):
    d, n = w.shape
    w = w.reshape(d, n // LANES, 2, 2, HEAD_DIM // 2)
    return w.transpose(0, 1, 3, 2, 4).reshape(d, n)


def kernel(x, c, positions, w_ada, b_ada, norm_pre, w_in, lambda_q1, lambda_k1, lambda_q2, lambda_k2,
           diff_subln, w_proj_a, w_proj_b, w_out, norm_post):
    assert x.shape == (BATCH, SEQ, D_MODEL) and w_in.shape == (1, D_MODEL, IN_WIDTH)
    x2d = x.reshape(BATCH * SEQ, D_MODEL)
    pos2d = positions.reshape(BATCH * SEQ, 1)
    half = HEAD_DIM // 2
    inv_freq = ROPE_THETA ** (-jnp.arange(half, dtype=F32) / half)
    inv_freq = jnp.tile(inv_freq, LANES // half).reshape(1, LANES)

    mod = _adaln(c, w_ada[0], b_ada)
    mod4 = mod.reshape(BATCH, 3, 1, D_MODEL)
    w = w_in[0]
    q_end, v_end, kd_end = VA_T * IN_TILE_N, QD_T * IN_TILE_N, VD_T * IN_TILE_N
    w = jnp.concatenate([_pair_rope_halves(w[:, :q_end]), w[:, q_end:v_end],
                         _pair_rope_halves(w[:, v_end:kd_end]), w[:, kd_end:]], axis=1)
    proj = _inproj(x2d, pos2d, mod4, norm_pre, inv_freq, w.astype(BF16))
    ya = _dsw_attention(proj)
    yd = _diff_attention(proj, lambda_q1, lambda_k1, lambda_q2, lambda_k2, diff_subln)
    out = _out_stage(ya, yd, proj, x2d, mod4, w_proj_a[0].astype(BF16), w_proj_b[0].astype(BF16),
                     w_out[0].astype(BF16), norm_post)
    return out.reshape(BATCH, SEQ, D_MODEL)
```

```python
import functools
import math

import jax
import jax.numpy as jnp
from jax import lax
from jax.experimental import pallas as pl
from jax.experimental.pallas import tpu as pltpu

F32 = jnp.float32
BF16 = jnp.bfloat16

D_MODEL = 1024
BATCH = 16
SEQ = 2048
HEAD_DIM = 64
ROPE_THETA = 10000.0
EPS = 1e-6
NEG_INF = -1e30
DSW_DILATIONS = (1, 4, 16)
DSW_RADIUS = 64
DSW_HEADS_PER_GROUP = 8
DIFF_HEADS = 8
LAMBDA_INIT = 0.8 - 0.6 * math.exp(-0.3 * 0)

LANES = 128
IN_TILE_N = 512
IN_CHUNK_M = 256
Q_SCALE = HEAD_DIM ** -0.5 * math.log2(math.e)
QA_T, KA_T, VA_T, ZA_T, QD_T, KD_T, VD_T, ZD_T, GM_T, END_T = 0, 3, 6, 9, 10, 12, 14, 16, 18, 22
IN_WIDTH = END_T * IN_TILE_N
QA_B, KA_B, VA_B, ZA_B, QD_B, KD_B, VD_B, ZD_B = 0, 12, 24, 36, 40, 48, 56, 64
DSW_CHUNK = 128
DIFF_TQ = 256
DIFF_TK = 256
SAFE_DEN_MIN = 2.0 ** -90
SAFE_DEN_MAX = 2.0 ** 115
OUT_TM = 1024
VMEM_LIMIT = 56 * 1024 * 1024


def _sigmoid(z):
    return 1.0 / (1.0 + jnp.exp(-z))


def _adaln_kernel(c_ref, w_ref, b_ref, o_ref):
    c = c_ref[...]
    sc = c * _sigmoid(c)
    o_ref[...] = jnp.dot(sc, w_ref[...], preferred_element_type=F32,
                         precision=lax.Precision.HIGHEST) + b_ref[...]


def _adaln(c, w_ada, b_ada):
    n_tiles = 3
    return pl.pallas_call(
        _adaln_kernel,
        grid=(n_tiles,),
        in_specs=[pl.BlockSpec((BATCH, D_MODEL), lambda n: (0, 0)),
                  pl.BlockSpec((D_MODEL, D_MODEL), lambda n: (0, n)),
                  pl.BlockSpec((1, D_MODEL), lambda n: (0, n))],
        out_specs=pl.BlockSpec((BATCH, D_MODEL), lambda n: (0, n)),
        out_shape=jax.ShapeDtypeStruct((BATCH, 3 * D_MODEL), F32),
        compiler_params=pltpu.CompilerParams(dimension_semantics=("arbitrary",),
                                             vmem_limit_bytes=VMEM_LIMIT),
        name="adaln",
    )(c, w_ada, b_ada)


def _inproj_kernel(x_ref, pos_ref, shift_ref, scale_ref, npre_ref, invf_ref, w_ref, o_ref,
                   h_scr, cos_scr, sin_scr):
    n = pl.program_id(1)

    @pl.when(n == 0)
    def _():
        rows = 256

        def body(i, carry):
            r = pl.multiple_of(i * rows, rows)
            x = x_ref[pl.ds(r, rows), :]
            ms = jnp.mean(x * x, axis=-1, keepdims=True)
            y = x * lax.rsqrt(ms + EPS) * npre_ref[...]
            h = y * (1.0 + scale_ref[...]) + shift_ref[...]
            h_scr[pl.ds(r, rows), :] = h.astype(BF16)
            ang = pos_ref[pl.ds(r, rows), :].astype(F32) * invf_ref[...]
            lane = lax.broadcasted_iota(jnp.int32, (1, LANES), 1)
            sgn = jnp.where(lane < LANES // 2, -1.0, 1.0)
            cos_scr[pl.ds(r, rows), :] = jnp.cos(ang)
            sin_scr[pl.ds(r, rows), :] = jnp.sin(ang) * sgn
            return carry

        lax.fori_loop(0, SEQ // rows, body, 0)

    is_q = (n < KA_T) | ((n >= QD_T) & (n < KD_T))
    is_rope = (n < VA_T) | ((n >= QD_T) & (n < VD_T))

    def chunks(epilogue):
        for c in range(SEQ // IN_CHUNK_M):
            rows = slice(c * IN_CHUNK_M, (c + 1) * IN_CHUNK_M)
            acc = jnp.dot(h_scr[rows, :], w_ref[...], preferred_element_type=F32)
            epilogue(rows, acc)

    @pl.when(is_rope)
    def _():
        qscale = jnp.where(is_q, Q_SCALE, 1.0).astype(F32)

        def rope(rows, acc):
            cos = cos_scr[rows, :] * qscale
            sin = sin_scr[rows, :] * qscale
            for j in range(IN_TILE_N // LANES):
                a = acc[:, j * LANES:(j + 1) * LANES]
                o_ref[rows, j * LANES:(j + 1) * LANES] = (
                    a * cos + pltpu.roll(a, LANES // 2, 1) * sin).astype(BF16)

        chunks(rope)

    @pl.when(jnp.logical_not(is_rope))
    def _():
        def plain(rows, acc):
            o_ref[rows, :] = acc.astype(BF16)

        chunks(plain)


def _inproj(x2d, pos2d, mod4, norm_pre, inv_freq, w_in_bf16):
    return pl.pallas_call(
        _inproj_kernel,
        grid=(BATCH, END_T),
        in_specs=[pl.BlockSpec((SEQ, D_MODEL), lambda b, n: (b, 0)),
                  pl.BlockSpec((SEQ, 1), lambda b, n: (b, 0)),
                  pl.BlockSpec((None, None, 1, D_MODEL), lambda b, n: (b, 0, 0, 0)),
                  pl.BlockSpec((None, None, 1, D_MODEL), lambda b, n: (b, 1, 0, 0)),
                  pl.BlockSpec((1, D_MODEL), lambda b, n: (0, 0)),
                  pl.BlockSpec((1, LANES), lambda b, n: (0, 0)),
                  pl.BlockSpec((D_MODEL, IN_TILE_N), lambda b, n: (0, n))],
        out_specs=pl.BlockSpec((SEQ, IN_TILE_N), lambda b, n: (b, n)),
        out_shape=jax.ShapeDtypeStruct((BATCH * SEQ, IN_WIDTH), BF16),
        scratch_shapes=[pltpu.VMEM((SEQ, D_MODEL), BF16),
                        pltpu.VMEM((SEQ, LANES), F32),
                        pltpu.VMEM((SEQ, LANES), F32)],
        compiler_params=pltpu.CompilerParams(dimension_semantics=("arbitrary", "arbitrary"),
                                             vmem_limit_bytes=VMEM_LIMIT),
        name="inproj",
    )(x2d, pos2d, mod4, mod4, norm_pre, inv_freq, w_in_bf16)


def _dsw_kernel(q0_ref, k0_ref, v0_ref, q1_ref, k1_ref, v1_ref, q2_ref, k2_ref, v2_ref,
                za_ref, o_ref, qf_scr, kf_scr, vf_scr, acc_scr, num_scr, max_scr, den_scr):
    lane = lax.broadcasted_iota(jnp.int32, (1, LANES), 1)
    lo = lane < HEAD_DIM
    qk_lo = (lane % HEAD_DIM) < HEAD_DIM // 2
    groups = ((q0_ref, k0_ref, v0_ref), (q1_ref, k1_ref, v1_ref), (q2_ref, k2_ref, v2_ref))
    nt = (((1,), (1,)), ((), ()))
    comb_rows = 256

    def geometry(g):
        dil = DSW_DILATIONS[g]
        sub_len = SEQ // dil
        win = min(2 * DSW_CHUNK, sub_len)
        return dil, sub_len, win

    def band_mask(q_start, k_start, win):
        qi = lax.broadcasted_iota(jnp.int32, (DSW_CHUNK, win), 0) + q_start
        kj = lax.broadcasted_iota(jnp.int32, (DSW_CHUNK, win), 1) + k_start
        return jnp.abs(kj - qi) <= DSW_RADIUS

    for g, (q_ref, k_ref, v_ref) in enumerate(groups):
        dil, sub_len, win = geometry(g)
        if dil > 1:
            qf_scr[...] = q_ref[...].astype(F32)
            kf_scr[...] = k_ref[...].astype(F32)
            vf_scr[...] = v_ref[...].astype(F32)
        for res in range(dil):
            for c in range(sub_len // DSW_CHUNK):
                q_start = c * DSW_CHUNK
                k_start = min(max(q_start - DSW_RADIUS, 0), sub_len - win)
                if dil == 1:
                    q = q_ref[q_start:q_start + DSW_CHUNK, :]
                    kw = k_ref[k_start:k_start + win, :]
                    vw = v_ref[k_start:k_start + win, :]
                    rows = pl.ds(q_start, DSW_CHUNK)
                else:
                    rows = pl.ds(res + dil * q_start, DSW_CHUNK, stride=dil)
                    krows = pl.ds(res + dil * k_start, win, stride=dil)
                    q = qf_scr[rows, :].astype(BF16)
                    kw = kf_scr[krows, :].astype(BF16)
                    vw = vf_scr[krows, :].astype(BF16)
                zq = jnp.zeros_like(q)
                q2 = jnp.concatenate([jnp.where(qk_lo, q, zq), jnp.where(qk_lo, zq, q)], axis=0)
                s = lax.dot_general(q2, kw, nt, preferred_element_type=F32)
                band = band_mask(q_start, k_start, win)
                e = jnp.where(jnp.concatenate([band, band], axis=0), jnp.exp2(s), 0.0).astype(BF16)
                one = jnp.ones_like(vw)
                r_a = jnp.dot(e[:DSW_CHUNK], jnp.where(lo, vw, one), preferred_element_type=F32)
                r_b = jnp.dot(e[DSW_CHUNK:], jnp.where(lo, one, vw), preferred_element_type=F32)
                if g == 0:
                    acc_scr[0, rows, :] = r_a
                    acc_scr[1, rows, :] = r_b
                else:
                    acc_scr[0, rows, :] = acc_scr[0, rows, :] + r_a
                    acc_scr[1, rows, :] = acc_scr[1, rows, :] + r_b

    def combine_unshifted(i, carry):
        d_min, d_max = carry
        r = pl.ds(pl.multiple_of(i * comb_rows, comb_rows), comb_rows)
        acc_a, acc_b = acc_scr[0, r, :], acc_scr[1, r, :]
        num = jnp.where(lo, acc_a, acc_b)
        den = pltpu.roll(jnp.where(lo, acc_b, acc_a), HEAD_DIM, 1)
        z = za_ref[r, :].astype(F32)
        o_ref[r, :] = (num / den * (z * _sigmoid(z))).astype(BF16)
        return (jnp.minimum(d_min, jnp.min(den, axis=0, keepdims=True)),
                jnp.maximum(d_max, jnp.max(den, axis=0, keepdims=True)))

    d_min, d_max = lax.fori_loop(
        0, SEQ // comb_rows, combine_unshifted,
        (jnp.full((1, LANES), SAFE_DEN_MAX, F32), jnp.full((1, LANES), SAFE_DEN_MIN, F32)))
    in_range = (jnp.min(d_min) >= SAFE_DEN_MIN) & (jnp.max(d_max) <= SAFE_DEN_MAX)

    @pl.when(jnp.logical_not(in_range))
    def _():
        for g, (q_ref, k_ref, v_ref) in enumerate(groups):
            dil, sub_len, win = geometry(g)
            n_c = sub_len // DSW_CHUNK
            qf_scr[...] = q_ref[...].astype(F32)
            kf_scr[...] = k_ref[...].astype(F32)
            vf_scr[...] = v_ref[...].astype(F32)

            def chunk(t, carry, g=g, dil=dil, sub_len=sub_len, win=win, n_c=n_c):
                res = t // n_c
                q_start = (t % n_c) * DSW_CHUNK
                k_start = jnp.clip(q_start - DSW_RADIUS, 0, sub_len - win)
                rows = pl.ds(res + dil * q_start, DSW_CHUNK, stride=dil)
                krows = pl.ds(res + dil * k_start, win, stride=dil)
                q = qf_scr[rows, :].astype(BF16)
                kw = kf_scr[krows, :].astype(BF16)
                vw = vf_scr[krows, :].astype(BF16)
                zero = jnp.zeros_like(kw)
                band = band_mask(q_start, k_start, win)
                s_a = lax.dot_general(q, jnp.where(qk_lo, kw, zero), nt, preferred_element_type=F32)
                s_b = lax.dot_general(q, jnp.where(qk_lo, zero, kw), nt, preferred_element_type=F32)
                s_a = jnp.where(band, s_a, NEG_INF)
                s_b = jnp.where(band, s_b, NEG_INF)
                m_a = jnp.max(s_a, axis=-1, keepdims=True)
                m_b = jnp.max(s_b, axis=-1, keepdims=True)
                e_a = jnp.exp2(s_a - m_a)
                e_b = jnp.exp2(s_b - m_b)
                d_a = jnp.sum(e_a, axis=-1, keepdims=True)
                d_b = jnp.sum(e_b, axis=-1, keepdims=True)
                num = (jnp.dot(e_a.astype(BF16), jnp.where(lo, vw, zero), preferred_element_type=F32)
                       + jnp.dot(e_b.astype(BF16), jnp.where(lo, zero, vw), preferred_element_type=F32))
                num_scr[g, rows, :] = num
                max_scr[g, rows, :] = jnp.where(lo, m_a, m_b)
                den_scr[g, rows, :] = jnp.where(lo, d_a, d_b)
                return carry

            lax.fori_loop(0, dil * n_c, chunk, 0)

        def combine(i, carry):
            r = pl.ds(pl.multiple_of(i * comb_rows, comb_rows), comb_rows)
            m0, m1, m2 = max_scr[0, r, :], max_scr[1, r, :], max_scr[2, r, :]
            m = jnp.maximum(jnp.maximum(m0, m1), m2)
            w0, w1, w2 = jnp.exp2(m0 - m), jnp.exp2(m1 - m), jnp.exp2(m2 - m)
            num = w0 * num_scr[0, r, :] + w1 * num_scr[1, r, :] + w2 * num_scr[2, r, :]
            den = w0 * den_scr[0, r, :] + w1 * den_scr[1, r, :] + w2 * den_scr[2, r, :]
            z = za_ref[r, :].astype(F32)
            o_ref[r, :] = (num / den * (z * _sigmoid(z))).astype(BF16)
            return carry

        lax.fori_loop(0, SEQ // comb_rows, combine, 0)


def _dsw_attention(proj):
    def col(base, g):
        return pl.BlockSpec((SEQ, LANES), lambda b, hp: (b, base + 4 * g + hp))

    in_specs = []
    for g in range(3):
        in_specs += [col(QA_B, g), col(KA_B, g), col(VA_B, g)]
    in_specs.append(pl.BlockSpec((SEQ, LANES), lambda b, hp: (b, ZA_B + hp)))
    return pl.pallas_call(
        _dsw_kernel,
        grid=(BATCH, DSW_HEADS_PER_GROUP // 2),
        in_specs=in_specs,
        out_specs=pl.BlockSpec((SEQ, LANES), lambda b, hp: (b, hp)),
        out_shape=jax.ShapeDtypeStruct((BATCH * SEQ, DSW_HEADS_PER_GROUP * HEAD_DIM), BF16),
        scratch_shapes=[pltpu.VMEM((SEQ, LANES), F32)] * 3 + [pltpu.VMEM((2, SEQ, LANES), F32)]
        + [pltpu.VMEM((3, SEQ, LANES), F32)] * 3,
        compiler_params=pltpu.CompilerParams(dimension_semantics=("arbitrary", "arbitrary"),
                                             vmem_limit_bytes=VMEM_LIMIT),
        name="dsw_attn",
    )(*([proj] * 10))


def _diff_kernel(q_ref, k_ref, v_ref, z_ref, lq1_ref, lk1_ref, lq2_ref, lk2_ref, subln_ref, o_ref,
                 klo_scr, khi_scr, e_scr, d_scr):
    lane = lax.broadcasted_iota(jnp.int32, (1, LANES), 1)
    lo = (lane % HEAD_DIM) < HEAD_DIM // 2
    k = k_ref[...]
    zero = jnp.zeros_like(k)
    klo_scr[...] = jnp.where(lo, k, zero)
    khi_scr[...] = jnp.where(lo, zero, k)
    lam = (jnp.exp(jnp.sum(lq1_ref[...] * lk1_ref[...], axis=-1, keepdims=True))
           - jnp.exp(jnp.sum(lq2_ref[...] * lk2_ref[...], axis=-1, keepdims=True)) + LAMBDA_INIT)
    nt = (((1,), (1,)), ((), ()))
    n_kt = SEQ // DIFF_TK
    n_chunks = SEQ // DIFF_TQ

    def rows_of(c):
        return pl.ds(pl.multiple_of(c * DIFF_TQ, DIFF_TQ), DIFF_TQ)

    def lane_partial_sum(e, acc):
        for j in range(e.shape[1] // LANES):
            part = e[:, j * LANES:(j + 1) * LANES]
            acc = part if acc is None else acc + part
        return acc

    def weights_unshifted(c, slot):
        q = q_ref[rows_of(c), :]
        for half, k_scr in enumerate((klo_scr, khi_scr)):
            den = None
            for kt in range(n_kt):
                ks = slice(kt * DIFF_TK, (kt + 1) * DIFF_TK)
                e = jnp.exp2(lax.dot_general(q, k_scr[ks, :], nt, preferred_element_type=F32))
                e_scr[slot, half, :, ks] = e.astype(BF16)
                den = lane_partial_sum(e, den)
            d_scr[slot, half] = den

    def weights_max_shifted(c, slot):
        q = q_ref[rows_of(c), :]
        for half, k_scr in enumerate((klo_scr, khi_scr)):
            s = lax.dot_general(q, k_scr[...], nt, preferred_element_type=F32)
            e = jnp.exp2(s - jnp.max(s, axis=-1, keepdims=True))
            e_scr[slot, half] = e.astype(BF16)
            d_scr[slot, half] = lane_partial_sum(e, None)

    def normalise_pv(c, slot):
        den_a = jnp.sum(d_scr[slot, 0], axis=-1, keepdims=True)
        den_b = jnp.sum(d_scr[slot, 1], axis=-1, keepdims=True)
        inv_a = (1.0 / den_a).astype(BF16)
        inv_b = (lam / den_b).astype(BF16)
        o = None
        for kt in range(n_kt):
            ks = slice(kt * DIFF_TK, (kt + 1) * DIFF_TK)
            a = e_scr[slot, 0, :, ks] * inv_a - e_scr[slot, 1, :, ks] * inv_b
            pv = jnp.dot(a, v_ref[ks, :], preferred_element_type=F32)
            o = pv if o is None else o + pv
        y = o * lax.rsqrt(jnp.mean(o * o, axis=-1, keepdims=True) + EPS) * subln_ref[...]
        y = y * (1.0 - LAMBDA_INIT)
        z = z_ref[rows_of(c), :].astype(F32)
        o_ref[rows_of(c), :] = (y * (z * _sigmoid(z))).astype(BF16)
        return (jnp.min(jnp.minimum(den_a, den_b), axis=0, keepdims=True),
                jnp.max(jnp.maximum(den_a, den_b), axis=0, keepdims=True))

    weights_unshifted(0, 0)

    def body(j, carry):
        c0 = 2 * j
        weights_unshifted(c0 + 1, 1)
        lo0, hi0 = normalise_pv(c0, 0)
        weights_unshifted(jnp.minimum(c0 + 2, n_chunks - 1), 0)
        lo1, hi1 = normalise_pv(c0 + 1, 1)
        return (jnp.minimum(carry[0], jnp.minimum(lo0, lo1)), jnp.maximum(carry[1], jnp.maximum(hi0, hi1)))

    d_min, d_max = lax.fori_loop(0, n_chunks // 2, body,
                                 (jnp.full((1, 1), SAFE_DEN_MAX, F32), jnp.full((1, 1), SAFE_DEN_MIN, F32)))
    in_range = (jnp.min(d_min) >= SAFE_DEN_MIN) & (jnp.max(d_max) <= SAFE_DEN_MAX)

    @pl.when(jnp.logical_not(in_range))
    def _():
        def body(c, carry):
            weights_max_shifted(c, 0)
            normalise_pv(c, 0)
            return carry

        lax.fori_loop(0, n_chunks, body, 0)


def _diff_attention(proj, lq1, lk1, lq2, lk2, subln):
    def col(base):
        return pl.BlockSpec((SEQ, LANES), lambda b, h: (b, base + h))

    small = pl.BlockSpec((1, HEAD_DIM), lambda b, h: (0, 0))
    return pl.pallas_call(
        _diff_kernel,
        grid=(BATCH, DIFF_HEADS),
        in_specs=[col(QD_B), col(KD_B), col(VD_B), col(ZD_B), small, small, small, small,
                  pl.BlockSpec((1, LANES), lambda b, h: (0, 0))],
        out_specs=pl.BlockSpec((SEQ, LANES), lambda b, h: (b, h)),
        out_shape=jax.ShapeDtypeStruct((BATCH * SEQ, DIFF_HEADS * LANES), BF16),
        scratch_shapes=[pltpu.VMEM((SEQ, LANES), BF16)] * 2
        + [pltpu.VMEM((2, 2, DIFF_TQ, SEQ), BF16),
           pltpu.VMEM((2, 2, DIFF_TQ, LANES), F32)],
        compiler_params=pltpu.CompilerParams(dimension_semantics=("arbitrary", "arbitrary"),
                                             vmem_limit_bytes=VMEM_LIMIT),
        name="diff_attn",
    )(proj, proj, proj, proj, lq1, lk1, lq2, lk2, subln)


def _out_kernel(ya_ref, yd_ref, ga_ref, gb_ref, x_ref, gate_ref, wpa_ref, wpb_ref, wout_ref, npost_ref,
                o_ref):
    pa = jnp.dot(ya_ref[...], wpa_ref[...], preferred_element_type=F32)
    pb = jnp.dot(yd_ref[...], wpb_ref[...], preferred_element_type=F32)
    mix = _sigmoid(ga_ref[...].astype(F32)) * pa + _sigmoid(gb_ref[...].astype(F32)) * pb
    y = jnp.dot(mix.astype(BF16), wout_ref[...], preferred_element_type=F32)
    y = y * lax.rsqrt(jnp.mean(y * y, axis=-1, keepdims=True) + EPS) * npost_ref[...]
    o_ref[...] = x_ref[...] + gate_ref[...] * y


def _out_stage(ya, yd, proj, x2d, mod4, wpa, wpb, wout, norm_post):
    per_batch = SEQ // OUT_TM
    gm_block = GM_T * IN_TILE_N // D_MODEL
    const = lambda i: (0, 0)
    return pl.pallas_call(
        _out_kernel,
        grid=(BATCH * per_batch,),
        in_specs=[pl.BlockSpec((OUT_TM, DSW_HEADS_PER_GROUP * HEAD_DIM), lambda i: (i, 0)),
                  pl.BlockSpec((OUT_TM, D_MODEL), lambda i: (i, 0)),
                  pl.BlockSpec((OUT_TM, D_MODEL), lambda i: (i, gm_block)),
                  pl.BlockSpec((OUT_TM, D_MODEL), lambda i: (i, gm_block + 1)),
                  pl.BlockSpec((OUT_TM, D_MODEL), lambda i: (i, 0)),
                  pl.BlockSpec((None, None, 1, D_MODEL), lambda i: (i // per_batch, 2, 0, 0)),
                  pl.BlockSpec((DSW_HEADS_PER_GROUP * HEAD_DIM, D_MODEL), const),
                  pl.BlockSpec((D_MODEL, D_MODEL), const),
                  pl.BlockSpec((D_MODEL, D_MODEL), const),
                  pl.BlockSpec((1, D_MODEL), const)],
        out_specs=pl.BlockSpec((OUT_TM, D_MODEL), lambda i: (i, 0)),
        out_shape=jax.ShapeDtypeStruct((BATCH * SEQ, D_MODEL), F32),
        compiler_params=pltpu.CompilerParams(dimension_semantics=("arbitrary",),
                                             vmem_limit_bytes=VMEM_LIMIT),
        name="out_stage",
    )(ya, yd, proj, proj, x2d, mod4, wpa, wpb, wout, norm_post)


def _pair_rope_halves(w):
    d, n = w.shape
    w = w.reshape(d, n // LANES, 2, 2, HEAD_DIM // 2)
    return w.transpose(0, 1, 3, 2, 4).reshape(d, n)


def kernel(x, c, positions, w_ada, b_ada, norm_pre, w_in, lambda_q1, lambda_k1, lambda_q2, lambda_k2,
           diff_subln, w_proj_a, w_proj_b, w_out, norm_post):
    assert x.shape == (BATCH, SEQ, D_MODEL) and w_in.shape == (1, D_MODEL, IN_WIDTH)
    x2d = x.reshape(BATCH * SEQ, D_MODEL)
    pos2d = positions.reshape(BATCH * SEQ, 1)
    half = HEAD_DIM // 2
    inv_freq = ROPE_THETA ** (-jnp.arange(half, dtype=F32) / half)
    inv_freq = jnp.tile(inv_freq, LANES // half).reshape(1, LANES)

    mod = _adaln(c, w_ada[0], b_ada)
    mod4 = mod.reshape(BATCH, 3, 1, D_MODEL)
    w = w_in[0]
    q_end, v_end, kd_end = VA_T * IN_TILE_N, QD_T * IN_TILE_N, VD_T * IN_TILE_N
    w = jnp.concatenate([_pair_rope_halves(w[:, :q_end]), w[:, q_end:v_end],
                         _pair_rope_halves(w[:, v_end:kd_end]), w[:, kd_end:]], axis=1)
    proj = _inproj(x2d, pos2d, mod4, norm_pre, inv_freq, w.astype(BF16))
    ya = _dsw_attention(proj)
    yd = _diff_attention(proj, lambda_q1, lambda_k1, lambda_q2, lambda_k2, diff_subln)
    out = _out_stage(ya, yd, proj, x2d, mod4, w_proj_a[0].astype(BF16), w_proj_b[0].astype(BF16),
                     w_out[0].astype(BF16), norm_post)
    return out.reshape(BATCH, SEQ, D_MODEL)
```

```python
import functools
import math

import jax
import jax.numpy as jnp
from jax import lax
from jax.experimental import pallas as pl
from jax.experimental.pallas import tpu as pltpu

F32 = jnp.float32
BF16 = jnp.bfloat16

D_MODEL = 1024
BATCH = 16
SEQ = 2048
HEAD_DIM = 64
ROPE_THETA = 10000.0
EPS = 1e-6
NEG_INF = -1e30
DSW_DILATIONS = (1, 4, 16)
DSW_RADIUS = 64
DSW_HEADS_PER_GROUP = 8
DIFF_HEADS = 8
LAMBDA_INIT = 0.8 - 0.6 * math.exp(-0.3 * 0)

LANES = 128
IN_TILE_N = 1024
IN_CHUNK_M = 128
ROPE_PACK = LANES // (HEAD_DIM // 2)
Q_SCALE = HEAD_DIM ** -0.5 * math.log2(math.e)
QA_COL, KA_COL, VA_COL, ZA_COL, QD_COL, KD_COL, VD_COL, ZD_COL, GM_COL, IN_WIDTH = (
    0, 1536, 3072, 4608, 5120, 6144, 7168, 8192, 9216, 11264)
IN_FIRST_PLAIN_TILE = VA_COL // IN_TILE_N
assert VA_COL % IN_TILE_N == 0 and QD_COL % IN_TILE_N == 0 and VD_COL % IN_TILE_N == 0
QA_B, KA_B, VA_B, ZA_B, QD_B, KD_B, VD_B, ZD_B = (
    c // LANES for c in (QA_COL, KA_COL, VA_COL, ZA_COL, QD_COL, KD_COL, VD_COL, ZD_COL))
DSW_CHUNK = 128
DIFF_TQ = 256
DIFF_TK = 256
SAFE_DEN_MIN = 2.0 ** -90
SAFE_DEN_MAX = 2.0 ** 115
OUT_TM = 1024
VMEM_LIMIT = 56 * 1024 * 1024


def _sigmoid(z):
    return 1.0 / (1.0 + jnp.exp(-z))


def _adaln_kernel(c_ref, w_ref, b_ref, o_ref):
    c = c_ref[...]
    sc = c * _sigmoid(c)
    o_ref[...] = jnp.dot(sc, w_ref[...], preferred_element_type=F32,
                         precision=lax.Precision.HIGHEST) + b_ref[...]


def _adaln(c, w_ada, b_ada):
    n_tiles = 3
    return pl.pallas_call(
        _adaln_kernel,
        grid=(n_tiles,),
        in_specs=[pl.BlockSpec((BATCH, D_MODEL), lambda n: (0, 0)),
                  pl.BlockSpec((D_MODEL, D_MODEL), lambda n: (0, n)),
                  pl.BlockSpec((1, D_MODEL), lambda n: (0, n))],
        out_specs=pl.BlockSpec((BATCH, D_MODEL), lambda n: (0, n)),
        out_shape=jax.ShapeDtypeStruct((BATCH, 3 * D_MODEL), F32),
        compiler_params=pltpu.CompilerParams(dimension_semantics=("arbitrary",),
                                             vmem_limit_bytes=VMEM_LIMIT),
        name="adaln",
    )(c, w_ada, b_ada)


def _inproj_tile(n):
    return jnp.where(n < 2, n + IN_FIRST_PLAIN_TILE, jnp.where(n < 2 + IN_FIRST_PLAIN_TILE, n - 2, n))


def _inproj_kernel(x_ref, pos_ref, shift_ref, scale_ref, npre_ref, invf_ref, w_ref, cs_ref, o_ref,
                   h_scr, cos_scr, sin_scr):
    n = pl.program_id(1)
    tile = _inproj_tile(n)
    is_rope = (n >= 2) & ((tile < IN_FIRST_PLAIN_TILE) | ((tile >= QD_COL // IN_TILE_N) & (tile < VD_COL // IN_TILE_N)))

    def chunks(before, epilogue):
        for c in range(SEQ // IN_CHUNK_M):
            rows = slice(c * IN_CHUNK_M, (c + 1) * IN_CHUNK_M)
            before(rows)
            acc = jnp.dot(h_scr[rows, :], w_ref[...], preferred_element_type=F32)
            epilogue(rows, acc)

    def nothing(rows):
        pass

    def plain(rows, acc):
        o_ref[rows, :] = acc.astype(BF16)

    def normalise_rows(rows):
        x = x_ref[rows, :]
        ms = jnp.mean(x * x, axis=-1, keepdims=True)
        y = x * lax.rsqrt(ms + EPS) * npre_ref[...]
        h_scr[rows, :] = (y * (1.0 + scale_ref[...]) + shift_ref[...]).astype(BF16)

    def rotary_tables(rows):
        n_packed = IN_CHUNK_M // ROPE_PACK
        ang = pos_ref[rows.start // ROPE_PACK:rows.start // ROPE_PACK + n_packed, :].astype(F32) * invf_ref[...]
        lane = lax.broadcasted_iota(jnp.int32, (1, LANES), 1)
        sgn = jnp.where(lane < LANES // 2, -1.0, 1.0)
        group = lane // (HEAD_DIM // 2)
        for table, dst, sign in ((jnp.cos(ang), cos_scr, None), (jnp.sin(ang), sin_scr, sgn)):
            for j in range(ROPE_PACK):
                own = jnp.where(group == j, table, 0.0)
                rep = own
                for shift in range(1, ROPE_PACK):
                    rep = rep + pltpu.roll(own, shift * (HEAD_DIM // 2), 1)
                if sign is not None:
                    rep = rep * sign
                dst[pl.ds(rows.start + j, n_packed, stride=ROPE_PACK), :] = rep

    def rope(rows, acc):
        cos, sin = cos_scr[rows, :], sin_scr[rows, :]
        for j in range(IN_TILE_N // LANES):
            a = acc[:, j * LANES:(j + 1) * LANES] * cs_ref[:, j * LANES:(j + 1) * LANES]
            o_ref[rows, j * LANES:(j + 1) * LANES] = (
                a * cos + pltpu.roll(a, LANES // 2, 1) * sin).astype(BF16)

    pl.when(n == 0)(lambda: chunks(normalise_rows, plain))
    pl.when(n == 1)(lambda: chunks(rotary_tables, plain))
    pl.when(is_rope)(lambda: chunks(nothing, rope))
    pl.when((n >= 2) & jnp.logical_not(is_rope))(lambda: chunks(nothing, plain))


def _inproj(x2d, pos2d, mod4, norm_pre, inv_freq, w_in_bf16, col_scale):
    return pl.pallas_call(
        _inproj_kernel,
        grid=(BATCH, IN_WIDTH // IN_TILE_N),
        in_specs=[pl.BlockSpec((SEQ, D_MODEL), lambda b, n: (b, 0)),
                  pl.BlockSpec((SEQ // ROPE_PACK, LANES), lambda b, n: (b, 0)),
                  pl.BlockSpec((None, None, 1, D_MODEL), lambda b, n: (b, 0, 0, 0)),
                  pl.BlockSpec((None, None, 1, D_MODEL), lambda b, n: (b, 1, 0, 0)),
                  pl.BlockSpec((1, D_MODEL), lambda b, n: (0, 0)),
                  pl.BlockSpec((1, LANES), lambda b, n: (0, 0)),
                  pl.BlockSpec((D_MODEL, IN_TILE_N), lambda b, n: (0, _inproj_tile(n))),
                  pl.BlockSpec((1, IN_TILE_N), lambda b, n: (0, _inproj_tile(n)))],
        out_specs=pl.BlockSpec((SEQ, IN_TILE_N), lambda b, n: (b, _inproj_tile(n))),
        out_shape=jax.ShapeDtypeStruct((BATCH * SEQ, IN_WIDTH), BF16),
        scratch_shapes=[pltpu.VMEM((SEQ, D_MODEL), BF16),
                        pltpu.VMEM((SEQ, LANES), F32),
                        pltpu.VMEM((SEQ, LANES), F32)],
        compiler_params=pltpu.CompilerParams(dimension_semantics=("arbitrary", "arbitrary"),
                                             vmem_limit_bytes=VMEM_LIMIT),
        name="inproj",
    )(x2d, pos2d, mod4, mod4, norm_pre, inv_freq, w_in_bf16, col_scale)


def _dsw_kernel(q0_ref, k0_ref, v0_ref, q1_ref, k1_ref, v1_ref, q2_ref, k2_ref, v2_ref,
                za_ref, o_ref, qf_scr, kf_scr, vf_scr, acc_scr, num_scr, max_scr, den_scr):
    lane = lax.broadcasted_iota(jnp.int32, (1, LANES), 1)
    lo = lane < HEAD_DIM
    qk_lo = (lane % HEAD_DIM) < HEAD_DIM // 2
    groups = ((q0_ref, k0_ref, v0_ref), (q1_ref, k1_ref, v1_ref), (q2_ref, k2_ref, v2_ref))
    nt = (((1,), (1,)), ((), ()))
    comb_rows = 256

    def geometry(g):
        dil = DSW_DILATIONS[g]
        sub_len = SEQ // dil
        win = min(2 * DSW_CHUNK, sub_len)
        return dil, sub_len, win

    def band_mask(q_start, k_start, win):
        qi = lax.broadcasted_iota(jnp.int32, (DSW_CHUNK, win), 0) + q_start
        kj = lax.broadcasted_iota(jnp.int32, (DSW_CHUNK, win), 1) + k_start
        return jnp.abs(kj - qi) <= DSW_RADIUS

    for g, (q_ref, k_ref, v_ref) in enumerate(groups):
        dil, sub_len, win = geometry(g)
        if dil > 1:
            qf_scr[...] = q_ref[...].astype(F32)
            kf_scr[...] = k_ref[...].astype(F32)
            vf_scr[...] = v_ref[...].astype(F32)
        for res in range(dil):
            for c in range(sub_len // DSW_CHUNK):
                q_start = c * DSW_CHUNK
                k_start = min(max(q_start - DSW_RADIUS, 0), sub_len - win)
                if dil == 1:
                    q = q_ref[q_start:q_start + DSW_CHUNK, :]
                    kw = k_ref[k_start:k_start + win, :]
                    vw = v_ref[k_start:k_start + win, :]
                    rows = pl.ds(q_start, DSW_CHUNK)
                else:
                    rows = pl.ds(res + dil * q_start, DSW_CHUNK, stride=dil)
                    krows = pl.ds(res + dil * k_start, win, stride=dil)
                    q = qf_scr[rows, :].astype(BF16)
                    kw = kf_scr[krows, :].astype(BF16)
                    vw = vf_scr[krows, :].astype(BF16)
                zq = jnp.zeros_like(q)
                q2 = jnp.concatenate([jnp.where(qk_lo, q, zq), jnp.where(qk_lo, zq, q)], axis=0)
                s = lax.dot_general(q2, kw, nt, preferred_element_type=F32)
                band = band_mask(q_start, k_start, win)
                e = jnp.where(jnp.concatenate([band, band], axis=0), jnp.exp2(s), 0.0).astype(BF16)
                one = jnp.ones_like(vw)
                r_a = jnp.dot(e[:DSW_CHUNK], jnp.where(lo, vw, one), preferred_element_type=F32)
                r_b = jnp.dot(e[DSW_CHUNK:], jnp.where(lo, one, vw), preferred_element_type=F32)
                if g == 0:
                    acc_scr[0, rows, :] = r_a
                    acc_scr[1, rows, :] = r_b
                else:
                    acc_scr[0, rows, :] = acc_scr[0, rows, :] + r_a
                    acc_scr[1, rows, :] = acc_scr[1, rows, :] + r_b

    def combine_unshifted(i, carry):
        d_min, d_max = carry
        r = pl.ds(pl.multiple_of(i * comb_rows, comb_rows), comb_rows)
        acc_a, acc_b = acc_scr[0, r, :], acc_scr[1, r, :]
        num = jnp.where(lo, acc_a, acc_b)
        den = pltpu.roll(jnp.where(lo, acc_b, acc_a), HEAD_DIM, 1)
        z = za_ref[r, :].astype(F32)
        o_ref[r, :] = (num / den * (z * _sigmoid(z))).astype(BF16)
        return (jnp.minimum(d_min, jnp.min(den, axis=0, keepdims=True)),
                jnp.maximum(d_max, jnp.max(den, axis=0, keepdims=True)))

    d_min, d_max = lax.fori_loop(
        0, SEQ // comb_rows, combine_unshifted,
        (jnp.full((1, LANES), SAFE_DEN_MAX, F32), jnp.full((1, LANES), SAFE_DEN_MIN, F32)))
    in_range = (jnp.min(d_min) >= SAFE_DEN_MIN) & (jnp.max(d_max) <= SAFE_DEN_MAX)

    @pl.when(jnp.logical_not(in_range))
    def _():
        for g, (q_ref, k_ref, v_ref) in enumerate(groups):
            dil, sub_len, win = geometry(g)
            n_c = sub_len // DSW_CHUNK
            qf_scr[...] = q_ref[...].astype(F32)
            kf_scr[...] = k_ref[...].astype(F32)
            vf_scr[...] = v_ref[...].astype(F32)

            def chunk(t, carry, g=g, dil=dil, sub_len=sub_len, win=win, n_c=n_c):
                res = t // n_c
                q_start = (t % n_c) * DSW_CHUNK
                k_start = jnp.clip(q_start - DSW_RADIUS, 0, sub_len - win)
                rows = pl.ds(res + dil * q_start, DSW_CHUNK, stride=dil)
                krows = pl.ds(res + dil * k_start, win, stride=dil)
                q = qf_scr[rows, :].astype(BF16)
                kw = kf_scr[krows, :].astype(BF16)
                vw = vf_scr[krows, :].astype(BF16)
                zero = jnp.zeros_like(kw)
                band = band_mask(q_start, k_start, win)
                s_a = lax.dot_general(q, jnp.where(qk_lo, kw, zero), nt, preferred_element_type=F32)
                s_b = lax.dot_general(q, jnp.where(qk_lo, zero, kw), nt, preferred_element_type=F32)
                s_a = jnp.where(band, s_a, NEG_INF)
                s_b = jnp.where(band, s_b, NEG_INF)
                m_a = jnp.max(s_a, axis=-1, keepdims=True)
                m_b = jnp.max(s_b, axis=-1, keepdims=True)
                e_a = jnp.exp2(s_a - m_a)
                e_b = jnp.exp2(s_b - m_b)
                d_a = jnp.sum(e_a, axis=-1, keepdims=True)
                d_b = jnp.sum(e_b, axis=-1, keepdims=True)
                num = (jnp.dot(e_a.astype(BF16), jnp.where(lo, vw, zero), preferred_element_type=F32)
                       + jnp.dot(e_b.astype(BF16), jnp.where(lo, zero, vw), preferred_element_type=F32))
                num_scr[g, rows, :] = num
                max_scr[g, rows, :] = jnp.where(lo, m_a, m_b)
                den_scr[g, rows, :] = jnp.where(lo, d_a, d_b)
                return carry

            lax.fori_loop(0, dil * n_c, chunk, 0)

        def combine(i, carry):
            r = pl.ds(pl.multiple_of(i * comb_rows, comb_rows), comb_rows)
            m0, m1, m2 = max_scr[0, r, :], max_scr[1, r, :], max_scr[2, r, :]
            m = jnp.maximum(jnp.maximum(m0, m1), m2)
            w0, w1, w2 = jnp.exp2(m0 - m), jnp.exp2(m1 - m), jnp.exp2(m2 - m)
            num = w0 * num_scr[0, r, :] + w1 * num_scr[1, r, :] + w2 * num_scr[2, r, :]
            den = w0 * den_scr[0, r, :] + w1 * den_scr[1, r, :] + w2 * den_scr[2, r, :]
            z = za_ref[r, :].astype(F32)
            o_ref[r, :] = (num / den * (z * _sigmoid(z))).astype(BF16)
            return carry

        lax.fori_loop(0, SEQ // comb_rows, combine, 0)


def _dsw_attention(proj):
    def col(base, g):
        return pl.BlockSpec((SEQ, LANES), lambda b, hp: (b, base + 4 * g + hp))

    in_specs = []
    for g in range(3):
        in_specs += [col(QA_B, g), col(KA_B, g), col(VA_B, g)]
    in_specs.append(pl.BlockSpec((SEQ, LANES), lambda b, hp: (b, ZA_B + hp)))
    return pl.pallas_call(
        _dsw_kernel,
        grid=(BATCH, DSW_HEADS_PER_GROUP // 2),
        in_specs=in_specs,
        out_specs=pl.BlockSpec((SEQ, LANES), lambda b, hp: (b, hp)),
        out_shape=jax.ShapeDtypeStruct((BATCH * SEQ, DSW_HEADS_PER_GROUP * HEAD_DIM), BF16),
        scratch_shapes=[pltpu.VMEM((SEQ, LANES), F32)] * 3 + [pltpu.VMEM((2, SEQ, LANES), F32)]
        + [pltpu.VMEM((3, SEQ, LANES), F32)] * 3,
        compiler_params=pltpu.CompilerParams(dimension_semantics=("arbitrary", "arbitrary"),
                                             vmem_limit_bytes=VMEM_LIMIT),
        name="dsw_attn",
    )(*([proj] * 10))


def _diff_kernel(q_ref, k_ref, v_ref, z_ref, lq1_ref, lk1_ref, lq2_ref, lk2_ref, subln_ref, o_ref,
                 klo_scr, khi_scr, e_scr, d_scr):
    lane = lax.broadcasted_iota(jnp.int32, (1, LANES), 1)
    lo = (lane % HEAD_DIM) < HEAD_DIM // 2
    k = k_ref[...]
    zero = jnp.zeros_like(k)
    klo_scr[...] = jnp.where(lo, k, zero)
    khi_scr[...] = jnp.where(lo, zero, k)
    lam = (jnp.exp(jnp.sum(lq1_ref[...] * lk1_ref[...], axis=-1, keepdims=True))
           - jnp.exp(jnp.sum(lq2_ref[...] * lk2_ref[...], axis=-1, keepdims=True)) + LAMBDA_INIT)
    nt = (((1,), (1,)), ((), ()))
    n_kt = SEQ // DIFF_TK
    n_chunks = SEQ // DIFF_TQ

    def rows_of(c):
        start = c * DIFF_TQ
        return pl.ds(start if isinstance(c, int) else pl.multiple_of(start, DIFF_TQ), DIFF_TQ)

    def lane_partial_sum(e, acc):
        for j in range(e.shape[1] // LANES):
            part = e[:, j * LANES:(j + 1) * LANES]
            acc = part if acc is None else acc + part
        return acc

    def weights_unshifted(c, slot):
        q = q_ref[rows_of(c), :]
        for half, k_scr in enumerate((klo_scr, khi_scr)):
            den = None
            for kt in range(n_kt):
                ks = slice(kt * DIFF_TK, (kt + 1) * DIFF_TK)
                e = jnp.exp2(lax.dot_general(q, k_scr[ks, :], nt, preferred_element_type=F32))
                e_scr[slot, half, :, ks] = e.astype(BF16)
                den = lane_partial_sum(e, den)
            d_scr[slot, half] = den

    def weights_max_shifted(c, slot):
        q = q_ref[rows_of(c), :]
        for half, k_scr in enumerate((klo_scr, khi_scr)):
            s = lax.dot_general(q, k_scr[...], nt, preferred_element_type=F32)
            e = jnp.exp2(s - jnp.max(s, axis=-1, keepdims=True))
            e_scr[slot, half] = e.astype(BF16)
            d_scr[slot, half] = lane_partial_sum(e, None)

    def normalise_pv(c, slot):
        den_a = jnp.sum(d_scr[slot, 0], axis=-1, keepdims=True)
        den_b = jnp.sum(d_scr[slot, 1], axis=-1, keepdims=True)
        inv_a = (1.0 / den_a).astype(BF16)
        inv_b = (lam / den_b).astype(BF16)
        o = None
        for kt in range(n_kt):
            ks = slice(kt * DIFF_TK, (kt + 1) * DIFF_TK)
            a = e_scr[slot, 0, :, ks] * inv_a - e_scr[slot, 1, :, ks] * inv_b
            pv = jnp.dot(a, v_ref[ks, :], preferred_element_type=F32)
            o = pv if o is None else o + pv
        y = o * lax.rsqrt(jnp.mean(o * o, axis=-1, keepdims=True) + EPS) * subln_ref[...]
        y = y * (1.0 - LAMBDA_INIT)
        z = z_ref[rows_of(c), :].astype(F32)
        o_ref[rows_of(c), :] = (y * (z * _sigmoid(z))).astype(BF16)
        return (jnp.min(jnp.minimum(den_a, den_b), axis=0, keepdims=True),
                jnp.max(jnp.maximum(den_a, den_b), axis=0, keepdims=True))

    weights_unshifted(0, 0)

    def chunk_pair(c0, carry, has_next):
        weights_unshifted(c0 + 1, 1)
        lo0, hi0 = normalise_pv(c0, 0)
        if has_next:
            weights_unshifted(c0 + 2, 0)
        lo1, hi1 = normalise_pv(c0 + 1, 1)
        return (jnp.minimum(carry[0], jnp.minimum(lo0, lo1)), jnp.maximum(carry[1], jnp.maximum(hi0, hi1)))

    carry = lax.fori_loop(0, n_chunks // 2 - 1, lambda j, carry: chunk_pair(2 * j, carry, True),
                          (jnp.full((1, 1), SAFE_DEN_MAX, F32), jnp.full((1, 1), SAFE_DEN_MIN, F32)))
    d_min, d_max = chunk_pair(n_chunks - 2, carry, False)
    in_range = (jnp.min(d_min) >= SAFE_DEN_MIN) & (jnp.max(d_max) <= SAFE_DEN_MAX)

    @pl.when(jnp.logical_not(in_range))
    def _():
        def body(c, carry):
            weights_max_shifted(c, 0)
            normalise_pv(c, 0)
            return carry

        lax.fori_loop(0, n_chunks, body, 0)


def _diff_attention(proj, lq1, lk1, lq2, lk2, subln):
    def col(base):
        return pl.BlockSpec((SEQ, LANES), lambda b, h: (b, base + h))

    small = pl.BlockSpec((1, HEAD_DIM), lambda b, h: (0, 0))
    return pl.pallas_call(
        _diff_kernel,
        grid=(BATCH, DIFF_HEADS),
        in_specs=[col(QD_B), col(KD_B), col(VD_B), col(ZD_B), small, small, small, small,
                  pl.BlockSpec((1, LANES), lambda b, h: (0, 0))],
        out_specs=pl.BlockSpec((SEQ, LANES), lambda b, h: (b, h)),
        out_shape=jax.ShapeDtypeStruct((BATCH * SEQ, DIFF_HEADS * LANES), BF16),
        scratch_shapes=[pltpu.VMEM((SEQ, LANES), BF16)] * 2
        + [pltpu.VMEM((2, 2, DIFF_TQ, SEQ), BF16),
           pltpu.VMEM((2, 2, DIFF_TQ, LANES), F32)],
        compiler_params=pltpu.CompilerParams(dimension_semantics=("arbitrary", "arbitrary"),
                                             vmem_limit_bytes=VMEM_LIMIT),
        name="diff_attn",
    )(proj, proj, proj, proj, lq1, lk1, lq2, lk2, subln)


def _out_kernel(ya_ref, yd_ref, ga_ref, gb_ref, x_ref, gate_ref, wpa_ref, wpb_ref, wout_ref, npost_ref,
                o_ref):
    pa = jnp.dot(ya_ref[...], wpa_ref[...], preferred_element_type=F32)
    pb = jnp.dot(yd_ref[...], wpb_ref[...], preferred_element_type=F32)
    mix = _sigmoid(ga_ref[...].astype(F32)) * pa + _sigmoid(gb_ref[...].astype(F32)) * pb
    y = jnp.dot(mix.astype(BF16), wout_ref[...], preferred_element_type=F32)
    y = y * lax.rsqrt(jnp.mean(y * y, axis=-1, keepdims=True) + EPS) * npost_ref[...]
    o_ref[...] = x_ref[...] + gate_ref[...] * y


def _out_stage(ya, yd, proj, x2d, mod4, wpa, wpb, wout, norm_post):
    per_batch = SEQ // OUT_TM
    gm_block = GM_COL // D_MODEL
    const = lambda i: (0, 0)
    return pl.pallas_call(
        _out_kernel,
        grid=(BATCH * per_batch,),
        in_specs=[pl.BlockSpec((OUT_TM, DSW_HEADS_PER_GROUP * HEAD_DIM), lambda i: (i, 0)),
                  pl.BlockSpec((OUT_TM, D_MODEL), lambda i: (i, 0)),
                  pl.BlockSpec((OUT_TM, D_MODEL), lambda i: (i, gm_block)),
                  pl.BlockSpec((OUT_TM, D_MODEL), lambda i: (i, gm_block + 1)),
                  pl.BlockSpec((OUT_TM, D_MODEL), lambda i: (i, 0)),
                  pl.BlockSpec((None, None, 1, D_MODEL), lambda i: (i // per_batch, 2, 0, 0)),
                  pl.BlockSpec((DSW_HEADS_PER_GROUP * HEAD_DIM, D_MODEL), const),
                  pl.BlockSpec((D_MODEL, D_MODEL), const),
                  pl.BlockSpec((D_MODEL, D_MODEL), const),
                  pl.BlockSpec((1, D_MODEL), const)],
        out_specs=pl.BlockSpec((OUT_TM, D_MODEL), lambda i: (i, 0)),
        out_shape=jax.ShapeDtypeStruct((BATCH * SEQ, D_MODEL), F32),
        compiler_params=pltpu.CompilerParams(dimension_semantics=("arbitrary",),
                                             vmem_limit_bytes=VMEM_LIMIT),
        name="out_stage",
    )(ya, yd, proj, proj, x2d, mod4, wpa, wpb, wout, norm_post)


def _pair_rope_halves(w):
    d, n = w.shape
    w = w.reshape(d, n // LANES, 2, 2, HEAD_DIM // 2)
    return w.transpose(0, 1, 3, 2, 4).reshape(d, n)


def kernel(x, c, positions, w_ada, b_ada, norm_pre, w_in, lambda_q1, lambda_k1, lambda_q2, lambda_k2,
           diff_subln, w_proj_a, w_proj_b, w_out, norm_post):
    assert x.shape == (BATCH, SEQ, D_MODEL) and w_in.shape == (1, D_MODEL, IN_WIDTH)
    x2d = x.reshape(BATCH * SEQ, D_MODEL)
    pos2d = jnp.repeat(positions.reshape(BATCH * SEQ // ROPE_PACK, ROPE_PACK), HEAD_DIM // 2, axis=1)
    half = HEAD_DIM // 2
    inv_freq = ROPE_THETA ** (-jnp.arange(half, dtype=F32) / half)
    inv_freq = jnp.tile(inv_freq, LANES // half).reshape(1, LANES)

    mod = _adaln(c, w_ada[0], b_ada)
    mod4 = mod.reshape(BATCH, 3, 1, D_MODEL)
    w = w_in[0]
    w = jnp.concatenate([_pair_rope_halves(w[:, :VA_COL]), w[:, VA_COL:QD_COL],
                         _pair_rope_halves(w[:, QD_COL:VD_COL]), w[:, VD_COL:]], axis=1)
    col = jnp.arange(IN_WIDTH)
    is_q_col = (col < KA_COL) | ((col >= QD_COL) & (col < KD_COL))
    col_scale = jnp.where(is_q_col, Q_SCALE, 1.0).astype(F32).reshape(1, IN_WIDTH)
    proj = _inproj(x2d, pos2d, mod4, norm_pre, inv_freq, w.astype(BF16), col_scale)
    ya = _dsw_attention(proj)
    yd = _diff_attention(proj, lambda_q1, lambda_k1, lambda_q2, lambda_k2, diff_subln)
    out = _out_stage(ya, yd, proj, x2d, mod4, w_proj_a[0].astype(BF16), w_proj_b[0].astype(BF16),
                     w_out[0].astype(BF16), norm_post)
    return out.reshape(BATCH, SEQ, D_MODEL)
```

```python
import functools
import math

import jax
import jax.numpy as jnp
from jax import lax
from jax.experimental import pallas as pl
from jax.experimental.pallas import tpu as pltpu

F32 = jnp.float32
BF16 = jnp.bfloat16

D_MODEL = 1024
BATCH = 16
SEQ = 2048
HEAD_DIM = 64
ROPE_THETA = 10000.0
EPS = 1e-6
NEG_INF = -1e30
DSW_DILATIONS = (1, 4, 16)
DSW_RADIUS = 64
DSW_HEADS_PER_GROUP = 8
DIFF_HEADS = 8
LAMBDA_INIT = 0.8 - 0.6 * math.exp(-0.3 * 0)

LANES = 128
IN_TILE_N = 1024
IN_CHUNK_M = 128
ROPE_PACK = LANES // (HEAD_DIM // 2)
Q_SCALE = HEAD_DIM ** -0.5 * math.log2(math.e)
QA_COL, KA_COL, VA_COL, ZA_COL, QD_COL, KD_COL, VD_COL, ZD_COL, GM_COL, IN_WIDTH = (
    0, 1536, 3072, 4608, 5120, 6144, 7168, 8192, 9216, 11264)
IN_FIRST_PLAIN_TILE = VA_COL // IN_TILE_N
assert VA_COL % IN_TILE_N == 0 and QD_COL % IN_TILE_N == 0 and VD_COL % IN_TILE_N == 0
QA_B, KA_B, VA_B, ZA_B, QD_B, KD_B, VD_B, ZD_B = (
    c // LANES for c in (QA_COL, KA_COL, VA_COL, ZA_COL, QD_COL, KD_COL, VD_COL, ZD_COL))
DSW_CHUNK = 128
DIFF_TQ = 256
DIFF_TK = 256
SAFE_DEN_MIN = 2.0 ** -90
SAFE_DEN_MAX = 2.0 ** 115
OUT_TM = 1024
VMEM_LIMIT = 56 * 1024 * 1024


def _sigmoid(z):
    return 1.0 / (1.0 + jnp.exp(-z))


def _adaln_kernel(c_ref, w_ref, b_ref, o_ref):
    c = c_ref[...]
    sc = c * _sigmoid(c)
    o_ref[...] = jnp.dot(sc, w_ref[...], preferred_element_type=F32,
                         precision=lax.Precision.HIGHEST) + b_ref[...]


def _adaln(c, w_ada, b_ada):
    n_tiles = 3
    return pl.pallas_call(
        _adaln_kernel,
        grid=(n_tiles,),
        in_specs=[pl.BlockSpec((BATCH, D_MODEL), lambda n: (0, 0)),
                  pl.BlockSpec((D_MODEL, D_MODEL), lambda n: (0, n)),
                  pl.BlockSpec((1, D_MODEL), lambda n: (0, n))],
        out_specs=pl.BlockSpec((BATCH, D_MODEL), lambda n: (0, n)),
        out_shape=jax.ShapeDtypeStruct((BATCH, 3 * D_MODEL), F32),
        compiler_params=pltpu.CompilerParams(dimension_semantics=("arbitrary",),
                                             vmem_limit_bytes=VMEM_LIMIT),
        name="adaln",
    )(c, w_ada, b_ada)


def _is_rope_tile(tile):
    return (tile < IN_FIRST_PLAIN_TILE) | ((tile >= QD_COL // IN_TILE_N) & (tile < VD_COL // IN_TILE_N))


def _wprep_kernel(w_ref, o_ref):
    @pl.when(_is_rope_tile(pl.program_id(0)))
    def _():
        group = lax.broadcasted_iota(jnp.int32, (1, LANES), 1) // (HEAD_DIM // 2)
        for j in range(IN_TILE_N // LANES):
            a = w_ref[:, j * LANES:(j + 1) * LANES]
            swapped = jnp.where(group == 1, pltpu.roll(a, LANES - HEAD_DIM // 2, 1),
                                jnp.where(group == 2, pltpu.roll(a, HEAD_DIM // 2, 1), a))
            o_ref[:, j * LANES:(j + 1) * LANES] = swapped.astype(BF16)

    @pl.when(jnp.logical_not(_is_rope_tile(pl.program_id(0))))
    def _():
        o_ref[...] = w_ref[...].astype(BF16)


def _prep_w_in(w_in):
    return pl.pallas_call(
        _wprep_kernel,
        grid=(IN_WIDTH // IN_TILE_N,),
        in_specs=[pl.BlockSpec((D_MODEL, IN_TILE_N), lambda t: (0, t))],
        out_specs=pl.BlockSpec((D_MODEL, IN_TILE_N), lambda t: (0, t)),
        out_shape=jax.ShapeDtypeStruct((D_MODEL, IN_WIDTH), BF16),
        compiler_params=pltpu.CompilerParams(dimension_semantics=("arbitrary",),
                                             vmem_limit_bytes=VMEM_LIMIT),
        name="wprep",
    )(w_in)


def _inproj_tile(n):
    return jnp.where(n < 2, n + IN_FIRST_PLAIN_TILE, jnp.where(n < 2 + IN_FIRST_PLAIN_TILE, n - 2, n))


def _inproj_kernel(x_ref, pos_ref, shift_ref, scale_ref, npre_ref, invf_ref, w_ref, cs_ref, o_ref,
                   h_scr, cos_scr, sin_scr):
    n = pl.program_id(1)
    tile = _inproj_tile(n)
    is_rope = (n >= 2) & _is_rope_tile(tile)

    def chunks(before, epilogue):
        for c in range(SEQ // IN_CHUNK_M):
            rows = slice(c * IN_CHUNK_M, (c + 1) * IN_CHUNK_M)
            before(rows)
            acc = jnp.dot(h_scr[rows, :], w_ref[...], preferred_element_type=F32)
            epilogue(rows, acc)

    def nothing(rows):
        pass

    def plain(rows, acc):
        o_ref[rows, :] = acc.astype(BF16)

    def normalise_rows(rows):
        x = x_ref[rows, :]
        ms = jnp.mean(x * x, axis=-1, keepdims=True)
        y = x * lax.rsqrt(ms + EPS) * npre_ref[...]
        h_scr[rows, :] = (y * (1.0 + scale_ref[...]) + shift_ref[...]).astype(BF16)

    def rotary_tables(rows):
        n_packed = IN_CHUNK_M // ROPE_PACK
        ang = pos_ref[rows.start // ROPE_PACK:rows.start // ROPE_PACK + n_packed, :].astype(F32) * invf_ref[...]
        lane = lax.broadcasted_iota(jnp.int32, (1, LANES), 1)
        sgn = jnp.where(lane < LANES // 2, -1.0, 1.0)
        group = lane // (HEAD_DIM // 2)
        for table, dst, sign in ((jnp.cos(ang), cos_scr, None), (jnp.sin(ang), sin_scr, sgn)):
            for j in range(ROPE_PACK):
                own = jnp.where(group == j, table, 0.0)
                rep = own
                for shift in range(1, ROPE_PACK):
                    rep = rep + pltpu.roll(own, shift * (HEAD_DIM // 2), 1)
                if sign is not None:
                    rep = rep * sign
                dst[pl.ds(rows.start + j, n_packed, stride=ROPE_PACK), :] = rep

    def rope(rows, acc):
        cos, sin = cos_scr[rows, :], sin_scr[rows, :]
        for j in range(IN_TILE_N // LANES):
            a = acc[:, j * LANES:(j + 1) * LANES] * cs_ref[:, j * LANES:(j + 1) * LANES]
            o_ref[rows, j * LANES:(j + 1) * LANES] = (
                a * cos + pltpu.roll(a, LANES // 2, 1) * sin).astype(BF16)

    pl.when(n == 0)(lambda: chunks(normalise_rows, plain))
    pl.when(n == 1)(lambda: chunks(rotary_tables, plain))
    pl.when(is_rope)(lambda: chunks(nothing, rope))
    pl.when((n >= 2) & jnp.logical_not(is_rope))(lambda: chunks(nothing, plain))


def _inproj(x2d, pos2d, mod4, norm_pre, inv_freq, w_in_bf16, col_scale):
    return pl.pallas_call(
        _inproj_kernel,
        grid=(BATCH, IN_WIDTH // IN_TILE_N),
        in_specs=[pl.BlockSpec((SEQ, D_MODEL), lambda b, n: (b, 0)),
                  pl.BlockSpec((SEQ // ROPE_PACK, LANES), lambda b, n: (b, 0)),
                  pl.BlockSpec((None, None, 1, D_MODEL), lambda b, n: (b, 0, 0, 0)),
                  pl.BlockSpec((None, None, 1, D_MODEL), lambda b, n: (b, 1, 0, 0)),
                  pl.BlockSpec((1, D_MODEL), lambda b, n: (0, 0)),
                  pl.BlockSpec((1, LANES), lambda b, n: (0, 0)),
                  pl.BlockSpec((D_MODEL, IN_TILE_N), lambda b, n: (0, _inproj_tile(n))),
                  pl.BlockSpec((1, IN_TILE_N), lambda b, n: (0, _inproj_tile(n)))],
        out_specs=pl.BlockSpec((SEQ, IN_TILE_N), lambda b, n: (b, _inproj_tile(n))),
        out_shape=jax.ShapeDtypeStruct((BATCH * SEQ, IN_WIDTH), BF16),
        scratch_shapes=[pltpu.VMEM((SEQ, D_MODEL), BF16),
                        pltpu.VMEM((SEQ, LANES), F32),
                        pltpu.VMEM((SEQ, LANES), F32)],
        compiler_params=pltpu.CompilerParams(dimension_semantics=("arbitrary", "arbitrary"),
                                             vmem_limit_bytes=VMEM_LIMIT),
        name="inproj",
    )(x2d, pos2d, mod4, mod4, norm_pre, inv_freq, w_in_bf16, col_scale)


def _dsw_kernel(q0_ref, k0_ref, v0_ref, q1_ref, k1_ref, v1_ref, q2_ref, k2_ref, v2_ref,
                za_ref, o_ref, qf_scr, kf_scr, vf_scr, acc_scr, num_scr, max_scr, den_scr):
    lane = lax.broadcasted_iota(jnp.int32, (1, LANES), 1)
    lo = lane < HEAD_DIM
    qk_lo = (lane % HEAD_DIM) < HEAD_DIM // 2
    groups = ((q0_ref, k0_ref, v0_ref), (q1_ref, k1_ref, v1_ref), (q2_ref, k2_ref, v2_ref))
    nt = (((1,), (1,)), ((), ()))
    comb_rows = 256

    def geometry(g):
        dil = DSW_DILATIONS[g]
        sub_len = SEQ // dil
        win = min(2 * DSW_CHUNK, sub_len)
        return dil, sub_len, win

    def band_mask(q_start, k_start, win):
        qi = lax.broadcasted_iota(jnp.int32, (DSW_CHUNK, win), 0) + q_start
        kj = lax.broadcasted_iota(jnp.int32, (DSW_CHUNK, win), 1) + k_start
        return jnp.abs(kj - qi) <= DSW_RADIUS

    for g, (q_ref, k_ref, v_ref) in enumerate(groups):
        dil, sub_len, win = geometry(g)
        if dil > 1:
            qf_scr[...] = q_ref[...].astype(F32)
            kf_scr[...] = k_ref[...].astype(F32)
            vf_scr[...] = v_ref[...].astype(F32)
        for res in range(dil):
            for c in range(sub_len // DSW_CHUNK):
                q_start = c * DSW_CHUNK
                k_start = min(max(q_start - DSW_RADIUS, 0), sub_len - win)
                if dil == 1:
                    q = q_ref[q_start:q_start + DSW_CHUNK, :]
                    kw = k_ref[k_start:k_start + win, :]
                    vw = v_ref[k_start:k_start + win, :]
                    rows = pl.ds(q_start, DSW_CHUNK)
                else:
                    rows = pl.ds(res + dil * q_start, DSW_CHUNK, stride=dil)
                    krows = pl.ds(res + dil * k_start, win, stride=dil)
                    q = qf_scr[rows, :].astype(BF16)
                    kw = kf_scr[krows, :].astype(BF16)
                    vw = vf_scr[krows, :].astype(BF16)
                zq = jnp.zeros_like(q)
                q2 = jnp.concatenate([jnp.where(qk_lo, q, zq), jnp.where(qk_lo, zq, q)], axis=0)
                s = lax.dot_general(q2, kw, nt, preferred_element_type=F32)
                band = band_mask(q_start, k_start, win)
                e = jnp.where(jnp.concatenate([band, band], axis=0), jnp.exp2(s), 0.0).astype(BF16)
                one = jnp.ones_like(vw)
                r_a = jnp.dot(e[:DSW_CHUNK], jnp.where(lo, vw, one), preferred_element_type=F32)
                r_b = jnp.dot(e[DSW_CHUNK:], jnp.where(lo, one, vw), preferred_element_type=F32)
                if g == 0:
                    acc_scr[0, rows, :] = r_a
                    acc_scr[1, rows, :] = r_b
                else:
                    acc_scr[0, rows, :] = acc_scr[0, rows, :] + r_a
                    acc_scr[1, rows, :] = acc_scr[1, rows, :] + r_b

    def combine_unshifted(i, carry):
        d_min, d_max = carry
        r = pl.ds(pl.multiple_of(i * comb_rows, comb_rows), comb_rows)
        acc_a, acc_b = acc_scr[0, r, :], acc_scr[1, r, :]
        num = jnp.where(lo, acc_a, acc_b)
        den = pltpu.roll(jnp.where(lo, acc_b, acc_a), HEAD_DIM, 1)
        z = za_ref[r, :].astype(F32)
        o_ref[r, :] = (num / den * (z * _sigmoid(z))).astype(BF16)
        return (jnp.minimum(d_min, jnp.min(den, axis=0, keepdims=True)),
                jnp.maximum(d_max, jnp.max(den, axis=0, keepdims=True)))

    d_min, d_max = lax.fori_loop(
        0, SEQ // comb_rows, combine_unshifted,
        (jnp.full((1, LANES), SAFE_DEN_MAX, F32), jnp.full((1, LANES), SAFE_DEN_MIN, F32)))
    in_range = (jnp.min(d_min) >= SAFE_DEN_MIN) & (jnp.max(d_max) <= SAFE_DEN_MAX)

    @pl.when(jnp.logical_not(in_range))
    def _():
        for g, (q_ref, k_ref, v_ref) in enumerate(groups):
            dil, sub_len, win = geometry(g)
            n_c = sub_len // DSW_CHUNK
            qf_scr[...] = q_ref[...].astype(F32)
            kf_scr[...] = k_ref[...].astype(F32)
            vf_scr[...] = v_ref[...].astype(F32)

            def chunk(t, carry, g=g, dil=dil, sub_len=sub_len, win=win, n_c=n_c):
                res = t // n_c
                q_start = (t % n_c) * DSW_CHUNK
                k_start = jnp.clip(q_start - DSW_RADIUS, 0, sub_len - win)
                rows = pl.ds(res + dil * q_start, DSW_CHUNK, stride=dil)
                krows = pl.ds(res + dil * k_start, win, stride=dil)
                q = qf_scr[rows, :].astype(BF16)
                kw = kf_scr[krows, :].astype(BF16)
                vw = vf_scr[krows, :].astype(BF16)
                zero = jnp.zeros_like(kw)
                band = band_mask(q_start, k_start, win)
                s_a = lax.dot_general(q, jnp.where(qk_lo, kw, zero), nt, preferred_element_type=F32)
                s_b = lax.dot_general(q, jnp.where(qk_lo, zero, kw), nt, preferred_element_type=F32)
                s_a = jnp.where(band, s_a, NEG_INF)
                s_b = jnp.where(band, s_b, NEG_INF)
                m_a = jnp.max(s_a, axis=-1, keepdims=True)
                m_b = jnp.max(s_b, axis=-1, keepdims=True)
                e_a = jnp.exp2(s_a - m_a)
                e_b = jnp.exp2(s_b - m_b)
                d_a = jnp.sum(e_a, axis=-1, keepdims=True)
                d_b = jnp.sum(e_b, axis=-1, keepdims=True)
                num = (jnp.dot(e_a.astype(BF16), jnp.where(lo, vw, zero), preferred_element_type=F32)
                       + jnp.dot(e_b.astype(BF16), jnp.where(lo, zero, vw), preferred_element_type=F32))
                num_scr[g, rows, :] = num
                max_scr[g, rows, :] = jnp.where(lo, m_a, m_b)
                den_scr[g, rows, :] = jnp.where(lo, d_a, d_b)
                return carry

            lax.fori_loop(0, dil * n_c, chunk, 0)

        def combine(i, carry):
            r = pl.ds(pl.multiple_of(i * comb_rows, comb_rows), comb_rows)
            m0, m1, m2 = max_scr[0, r, :], max_scr[1, r, :], max_scr[2, r, :]
            m = jnp.maximum(jnp.maximum(m0, m1), m2)
            w0, w1, w2 = jnp.exp2(m0 - m), jnp.exp2(m1 - m), jnp.exp2(m2 - m)
            num = w0 * num_scr[0, r, :] + w1 * num_scr[1, r, :] + w2 * num_scr[2, r, :]
            den = w0 * den_scr[0, r, :] + w1 * den_scr[1, r, :] + w2 * den_scr[2, r, :]
            z = za_ref[r, :].astype(F32)
            o_ref[r, :] = (num / den * (z * _sigmoid(z))).astype(BF16)
            return carry

        lax.fori_loop(0, SEQ // comb_rows, combine, 0)


def _dsw_attention(proj):
    def col(base, g):
        return pl.BlockSpec((SEQ, LANES), lambda b, hp: (b, base + 4 * g + hp))

    in_specs = []
    for g in range(3):
        in_specs += [col(QA_B, g), col(KA_B, g), col(VA_B, g)]
    in_specs.append(pl.BlockSpec((SEQ, LANES), lambda b, hp: (b, ZA_B + hp)))
    return pl.pallas_call(
        _dsw_kernel,
        grid=(BATCH, DSW_HEADS_PER_GROUP // 2),
        in_specs=in_specs,
        out_specs=pl.BlockSpec((SEQ, LANES), lambda b, hp: (b, hp)),
        out_shape=jax.ShapeDtypeStruct((BATCH * SEQ, DSW_HEADS_PER_GROUP * HEAD_DIM), BF16),
        scratch_shapes=[pltpu.VMEM((SEQ, LANES), F32)] * 3 + [pltpu.VMEM((2, SEQ, LANES), F32)]
        + [pltpu.VMEM((3, SEQ, LANES), F32)] * 3,
        compiler_params=pltpu.CompilerParams(dimension_semantics=("arbitrary", "arbitrary"),
                                             vmem_limit_bytes=VMEM_LIMIT),
        name="dsw_attn",
    )(*([proj] * 10))


def _diff_kernel(q_ref, k_ref, v_ref, z_ref, lq1_ref, lk1_ref, lq2_ref, lk2_ref, subln_ref, o_ref,
                 klo_scr, khi_scr, e_scr, d_scr):
    lane = lax.broadcasted_iota(jnp.int32, (1, LANES), 1)
    lo = (lane % HEAD_DIM) < HEAD_DIM // 2
    k = k_ref[...]
    zero = jnp.zeros_like(k)
    klo_scr[...] = jnp.where(lo, k, zero)
    khi_scr[...] = jnp.where(lo, zero, k)
    lam = (jnp.exp(jnp.sum(lq1_ref[...] * lk1_ref[...], axis=-1, keepdims=True))
           - jnp.exp(jnp.sum(lq2_ref[...] * lk2_ref[...], axis=-1, keepdims=True)) + LAMBDA_INIT)
    nt = (((1,), (1,)), ((), ()))
    n_kt = SEQ // DIFF_TK
    n_chunks = SEQ // DIFF_TQ

    def rows_of(c):
        start = c * DIFF_TQ
        return pl.ds(start if isinstance(c, int) else pl.multiple_of(start, DIFF_TQ), DIFF_TQ)

    def lane_partial_sum(e, acc):
        for j in range(e.shape[1] // LANES):
            part = e[:, j * LANES:(j + 1) * LANES]
            acc = part if acc is None else acc + part
        return acc

    def weights_unshifted(c, slot):
        q = q_ref[rows_of(c), :]
        for half, k_scr in enumerate((klo_scr, khi_scr)):
            den = None
            for kt in range(n_kt):
                ks = slice(kt * DIFF_TK, (kt + 1) * DIFF_TK)
                e = jnp.exp2(lax.dot_general(q, k_scr[ks, :], nt, preferred_element_type=F32))
                e_scr[slot, half, :, ks] = e.astype(BF16)
                den = lane_partial_sum(e, den)
            d_scr[slot, half] = den

    def weights_max_shifted(c, slot):
        q = q_ref[rows_of(c), :]
        for half, k_scr in enumerate((klo_scr, khi_scr)):
            s = lax.dot_general(q, k_scr[...], nt, preferred_element_type=F32)
            e = jnp.exp2(s - jnp.max(s, axis=-1, keepdims=True))
            e_scr[slot, half] = e.astype(BF16)
            d_scr[slot, half] = lane_partial_sum(e, None)

    def normalise_pv(c, slot):
        den_a = jnp.sum(d_scr[slot, 0], axis=-1, keepdims=True)
        den_b = jnp.sum(d_scr[slot, 1], axis=-1, keepdims=True)
        inv_a = (1.0 / den_a).astype(BF16)
        inv_b = (lam / den_b).astype(BF16)
        o = None
        for kt in range(n_kt):
            ks = slice(kt * DIFF_TK, (kt + 1) * DIFF_TK)
            a = e_scr[slot, 0, :, ks] * inv_a - e_scr[slot, 1, :, ks] * inv_b
            pv = jnp.dot(a, v_ref[ks, :], preferred_element_type=F32)
            o = pv if o is None else o + pv
        y = o * lax.rsqrt(jnp.mean(o * o, axis=-1, keepdims=True) + EPS) * subln_ref[...]
        y = y * (1.0 - LAMBDA_INIT)
        z = z_ref[rows_of(c), :].astype(F32)
        o_ref[rows_of(c), :] = (y * (z * _sigmoid(z))).astype(BF16)
        return (jnp.min(jnp.minimum(den_a, den_b), axis=0, keepdims=True),
                jnp.max(jnp.maximum(den_a, den_b), axis=0, keepdims=True))

    weights_unshifted(0, 0)

    def chunk_pair(c0, carry, has_next):
        weights_unshifted(c0 + 1, 1)
        lo0, hi0 = normalise_pv(c0, 0)
        if has_next:
            weights_unshifted(c0 + 2, 0)
        lo1, hi1 = normalise_pv(c0 + 1, 1)
        return (jnp.minimum(carry[0], jnp.minimum(lo0, lo1)), jnp.maximum(carry[1], jnp.maximum(hi0, hi1)))

    carry = (jnp.full((1, 1), SAFE_DEN_MAX, F32), jnp.full((1, 1), SAFE_DEN_MIN, F32))
    for c0 in range(0, n_chunks, 2):
        carry = chunk_pair(c0, carry, c0 + 2 < n_chunks)
    d_min, d_max = carry
    in_range = (jnp.min(d_min) >= SAFE_DEN_MIN) & (jnp.max(d_max) <= SAFE_DEN_MAX)

    @pl.when(jnp.logical_not(in_range))
    def _():
        def body(c, carry):
            weights_max_shifted(c, 0)
            normalise_pv(c, 0)
            return carry

        lax.fori_loop(0, n_chunks, body, 0)


def _diff_attention(proj, lq1, lk1, lq2, lk2, subln):
    def col(base):
        return pl.BlockSpec((SEQ, LANES), lambda b, h: (b, base + h))

    small = pl.BlockSpec((1, HEAD_DIM), lambda b, h: (0, 0))
    return pl.pallas_call(
        _diff_kernel,
        grid=(BATCH, DIFF_HEADS),
        in_specs=[col(QD_B), col(KD_B), col(VD_B), col(ZD_B), small, small, small, small,
                  pl.BlockSpec((1, LANES), lambda b, h: (0, 0))],
        out_specs=pl.BlockSpec((SEQ, LANES), lambda b, h: (b, h)),
        out_shape=jax.ShapeDtypeStruct((BATCH * SEQ, DIFF_HEADS * LANES), BF16),
        scratch_shapes=[pltpu.VMEM((SEQ, LANES), BF16)] * 2
        + [pltpu.VMEM((2, 2, DIFF_TQ, SEQ), BF16),
           pltpu.VMEM((2, 2, DIFF_TQ, LANES), F32)],
        compiler_params=pltpu.CompilerParams(dimension_semantics=("arbitrary", "arbitrary"),
                                             vmem_limit_bytes=VMEM_LIMIT),
        name="diff_attn",
    )(proj, proj, proj, proj, lq1, lk1, lq2, lk2, subln)


def _out_kernel(ya_ref, yd_ref, ga_ref, gb_ref, x_ref, gate_ref, wpa_ref, wpb_ref, wout_ref, npost_ref,
                o_ref):
    pa = jnp.dot(ya_ref[...], wpa_ref[...], preferred_element_type=F32)
    pb = jnp.dot(yd_ref[...], wpb_ref[...], preferred_element_type=F32)
    mix = _sigmoid(ga_ref[...].astype(F32)) * pa + _sigmoid(gb_ref[...].astype(F32)) * pb
    y = jnp.dot(mix.astype(BF16), wout_ref[...], preferred_element_type=F32)
    y = y * lax.rsqrt(jnp.mean(y * y, axis=-1, keepdims=True) + EPS) * npost_ref[...]
    o_ref[...] = x_ref[...] + gate_ref[...] * y


def _out_stage(ya, yd, proj, x2d, mod4, wpa, wpb, wout, norm_post):
    per_batch = SEQ // OUT_TM
    gm_block = GM_COL // D_MODEL
    const = lambda i: (0, 0)
    return pl.pallas_call(
        _out_kernel,
        grid=(BATCH * per_batch,),
        in_specs=[pl.BlockSpec((OUT_TM, DSW_HEADS_PER_GROUP * HEAD_DIM), lambda i: (i, 0)),
                  pl.BlockSpec((OUT_TM, D_MODEL), lambda i: (i, 0)),
                  pl.BlockSpec((OUT_TM, D_MODEL), lambda i: (i, gm_block)),
                  pl.BlockSpec((OUT_TM, D_MODEL), lambda i: (i, gm_block + 1)),
                  pl.BlockSpec((OUT_TM, D_MODEL), lambda i: (i, 0)),
                  pl.BlockSpec((None, None, 1, D_MODEL), lambda i: (i // per_batch, 2, 0, 0)),
                  pl.BlockSpec((DSW_HEADS_PER_GROUP * HEAD_DIM, D_MODEL), const),
                  pl.BlockSpec((D_MODEL, D_MODEL), const),
                  pl.BlockSpec((D_MODEL, D_MODEL), const),
                  pl.BlockSpec((1, D_MODEL), const)],
        out_specs=pl.BlockSpec((OUT_TM, D_MODEL), lambda i: (i, 0)),
        out_shape=jax.ShapeDtypeStruct((BATCH * SEQ, D_MODEL), F32),
        compiler_params=pltpu.CompilerParams(dimension_semantics=("arbitrary",),
                                             vmem_limit_bytes=VMEM_LIMIT),
        name="out_stage",
    )(ya, yd, proj, proj, x2d, mod4, wpa, wpb, wout, norm_post)


def kernel(x, c, positions, w_ada, b_ada, norm_pre, w_in, lambda_q1, lambda_k1, lambda_q2, lambda_k2,
           diff_subln, w_proj_a, w_proj_b, w_out, norm_post):
    assert x.shape == (BATCH, SEQ, D_MODEL) and w_in.shape == (1, D_MODEL, IN_WIDTH)
    x2d = x.reshape(BATCH * SEQ, D_MODEL)
    pos2d = jnp.repeat(positions.reshape(BATCH * SEQ // ROPE_PACK, ROPE_PACK), HEAD_DIM // 2, axis=1)
    half = HEAD_DIM // 2
    inv_freq = ROPE_THETA ** (-jnp.arange(half, dtype=F32) / half)
    inv_freq = jnp.tile(inv_freq, LANES // half).reshape(1, LANES)

    mod = _adaln(c, w_ada[0], b_ada)
    mod4 = mod.reshape(BATCH, 3, 1, D_MODEL)
    col = jnp.arange(IN_WIDTH)
    is_q_col = (col < KA_COL) | ((col >= QD_COL) & (col < KD_COL))
    col_scale = jnp.where(is_q_col, Q_SCALE, 1.0).astype(F32).reshape(1, IN_WIDTH)
    proj = _inproj(x2d, pos2d, mod4, norm_pre, inv_freq, _prep_w_in(w_in[0]), col_scale)
    ya = _dsw_attention(proj)
    yd = _diff_attention(proj, lambda_q1, lambda_k1, lambda_q2, lambda_k2, diff_subln)
    out = _out_stage(ya, yd, proj, x2d, mod4, w_proj_a[0].astype(BF16), w_proj_b[0].astype(BF16),
                     w_out[0].astype(BF16), norm_post)
    return out.reshape(BATCH, SEQ, D_MODEL)
```

```python
import functools
import math

import jax
import jax.numpy as jnp
from jax import lax
from jax.experimental import pallas as pl
from jax.experimental.pallas import tpu as pltpu

F32 = jnp.float32
BF16 = jnp.bfloat16

D_MODEL = 1024
BATCH = 16
SEQ = 2048
HEAD_DIM = 64
ROPE_THETA = 10000.0
EPS = 1e-6
NEG_INF = -1e30
DSW_DILATIONS = (1, 4, 16)
DSW_RADIUS = 64
DSW_HEADS_PER_GROUP = 8
DIFF_HEADS = 8
LAMBDA_INIT = 0.8 - 0.6 * math.exp(-0.3 * 0)

LANES = 128
SUBLANES = 8
IN_TILE_N = 1024
IN_CHUNK_M = 128
ROPE_PACK = LANES // (HEAD_DIM // 2)
Q_SCALE = HEAD_DIM ** -0.5 * math.log2(math.e)
QA_COL, KA_COL, VA_COL, ZA_COL, QD_COL, KD_COL, VD_COL, ZD_COL, GM_COL, IN_WIDTH = (
    0, 1536, 3072, 4608, 5120, 6144, 7168, 8192, 9216, 11264)
IN_FIRST_PLAIN_TILE = VA_COL // IN_TILE_N
assert VA_COL % IN_TILE_N == 0 and QD_COL % IN_TILE_N == 0 and VD_COL % IN_TILE_N == 0
QA_B, KA_B, VA_B, ZA_B, QD_B, KD_B, VD_B, ZD_B = (
    c // LANES for c in (QA_COL, KA_COL, VA_COL, ZA_COL, QD_COL, KD_COL, VD_COL, ZD_COL))
DSW_CHUNK = 128
DIFF_TQ = 256
DIFF_TK = 256
SAFE_DEN_MIN = 2.0 ** -90
SAFE_DEN_MAX = 2.0 ** 115
OUT_TM = 1024
VMEM_LIMIT = 56 * 1024 * 1024


def _sigmoid(z):
    return 1.0 / (1.0 + jnp.exp(-z))


def _adaln_kernel(c_ref, w_ref, b_ref, o_ref):
    c = c_ref[...]
    sc = c * _sigmoid(c)
    o_ref[...] = jnp.dot(sc, w_ref[...], preferred_element_type=F32,
                         precision=lax.Precision.HIGHEST) + b_ref[...]


def _adaln(c, w_ada, b_ada):
    n_tiles = 3
    return pl.pallas_call(
        _adaln_kernel,
        grid=(n_tiles,),
        in_specs=[pl.BlockSpec((BATCH, D_MODEL), lambda n: (0, 0)),
                  pl.BlockSpec((D_MODEL, D_MODEL), lambda n: (0, n)),
                  pl.BlockSpec((1, D_MODEL), lambda n: (0, n))],
        out_specs=pl.BlockSpec((BATCH, D_MODEL), lambda n: (0, n)),
        out_shape=jax.ShapeDtypeStruct((BATCH, 3 * D_MODEL), F32),
        compiler_params=pltpu.CompilerParams(dimension_semantics=("arbitrary",),
                                             vmem_limit_bytes=VMEM_LIMIT),
        name="adaln",
    )(c, w_ada, b_ada)


def _is_rope_tile(tile):
    return (tile < IN_FIRST_PLAIN_TILE) | ((tile >= QD_COL // IN_TILE_N) & (tile < VD_COL // IN_TILE_N))


def _wprep_kernel(w_ref, o_ref):
    @pl.when(_is_rope_tile(pl.program_id(0)))
    def _():
        group = lax.broadcasted_iota(jnp.int32, (1, LANES), 1) // (HEAD_DIM // 2)
        for j in range(IN_TILE_N // LANES):
            a = w_ref[:, j * LANES:(j + 1) * LANES]
            swapped = jnp.where(group == 1, pltpu.roll(a, LANES - HEAD_DIM // 2, 1),
                                jnp.where(group == 2, pltpu.roll(a, HEAD_DIM // 2, 1), a))
            o_ref[:, j * LANES:(j + 1) * LANES] = swapped.astype(BF16)

    @pl.when(jnp.logical_not(_is_rope_tile(pl.program_id(0))))
    def _():
        o_ref[...] = w_ref[...].astype(BF16)


def _prep_w_in(w_in):
    return pl.pallas_call(
        _wprep_kernel,
        grid=(IN_WIDTH // IN_TILE_N,),
        in_specs=[pl.BlockSpec((D_MODEL, IN_TILE_N), lambda t: (0, t))],
        out_specs=pl.BlockSpec((D_MODEL, IN_TILE_N), lambda t: (0, t)),
        out_shape=jax.ShapeDtypeStruct((D_MODEL, IN_WIDTH), BF16),
        compiler_params=pltpu.CompilerParams(dimension_semantics=("arbitrary",),
                                             vmem_limit_bytes=VMEM_LIMIT),
        name="wprep",
    )(w_in)


def _inproj_tile(n):
    return jnp.where(n < 2, n + IN_FIRST_PLAIN_TILE, jnp.where(n < 2 + IN_FIRST_PLAIN_TILE, n - 2, n))


def _inproj_kernel(x_ref, pos_ref, shift_ref, scale_ref, npre_ref, invf_ref, w_ref, cs_ref, o_ref,
                   h_scr, cos_scr, sin_scr):
    n = pl.program_id(1)
    tile = _inproj_tile(n)
    is_rope = (n >= 2) & _is_rope_tile(tile)

    def chunks(before, epilogue):
        for c in range(SEQ // IN_CHUNK_M):
            rows = slice(c * IN_CHUNK_M, (c + 1) * IN_CHUNK_M)
            before(rows)
            acc = jnp.dot(h_scr[rows, :], w_ref[...], preferred_element_type=F32)
            epilogue(rows, acc)

    def nothing(rows):
        pass

    def plain(rows, acc):
        o_ref[rows, :] = acc.astype(BF16)

    def normalise_rows(rows):
        x = x_ref[rows, :]
        ms = jnp.mean(x * x, axis=-1, keepdims=True)
        y = x * lax.rsqrt(ms + EPS) * npre_ref[...]
        h_scr[rows, :] = (y * (1.0 + scale_ref[...]) + shift_ref[...]).astype(BF16)

    def rotary_tables(rows):
        n_packed = IN_CHUNK_M // ROPE_PACK
        ang = pos_ref[rows.start // ROPE_PACK:rows.start // ROPE_PACK + n_packed, :].astype(F32) * invf_ref[...]
        lane = lax.broadcasted_iota(jnp.int32, (1, LANES), 1)
        sgn = jnp.where(lane < LANES // 2, -1.0, 1.0)
        group = lane // (HEAD_DIM // 2)
        for table, dst, sign in ((jnp.cos(ang), cos_scr, None), (jnp.sin(ang), sin_scr, sgn)):
            for j in range(ROPE_PACK):
                own = jnp.where(group == j, table, 0.0)
                rep = own
                for shift in range(1, ROPE_PACK):
                    rep = rep + pltpu.roll(own, shift * (HEAD_DIM // 2), 1)
                if sign is not None:
                    rep = rep * sign
                dst[pl.ds(rows.start + j, n_packed, stride=ROPE_PACK), :] = rep

    def rope(rows, acc):
        cos, sin = cos_scr[rows, :], sin_scr[rows, :]
        for j in range(IN_TILE_N // LANES):
            a = acc[:, j * LANES:(j + 1) * LANES] * cs_ref[:, j * LANES:(j + 1) * LANES]
            o_ref[rows, j * LANES:(j + 1) * LANES] = (
                a * cos + pltpu.roll(a, LANES // 2, 1) * sin).astype(BF16)

    pl.when(n == 0)(lambda: chunks(normalise_rows, plain))
    pl.when(n == 1)(lambda: chunks(rotary_tables, plain))
    pl.when(is_rope)(lambda: chunks(nothing, rope))
    pl.when((n >= 2) & jnp.logical_not(is_rope))(lambda: chunks(nothing, plain))


def _inproj(x2d, pos2d, mod4, norm_pre, inv_freq, w_in_bf16, col_scale):
    return pl.pallas_call(
        _inproj_kernel,
        grid=(BATCH, IN_WIDTH // IN_TILE_N),
        in_specs=[pl.BlockSpec((SEQ, D_MODEL), lambda b, n: (b, 0)),
                  pl.BlockSpec((SEQ // ROPE_PACK, LANES), lambda b, n: (b, 0)),
                  pl.BlockSpec((None, None, 1, D_MODEL), lambda b, n: (b, 0, 0, 0)),
                  pl.BlockSpec((None, None, 1, D_MODEL), lambda b, n: (b, 1, 0, 0)),
                  pl.BlockSpec((1, D_MODEL), lambda b, n: (0, 0)),
                  pl.BlockSpec((1, LANES), lambda b, n: (0, 0)),
                  pl.BlockSpec((D_MODEL, IN_TILE_N), lambda b, n: (0, _inproj_tile(n))),
                  pl.BlockSpec((1, IN_TILE_N), lambda b, n: (0, _inproj_tile(n)))],
        out_specs=pl.BlockSpec((SEQ, IN_TILE_N), lambda b, n: (b, _inproj_tile(n))),
        out_shape=jax.ShapeDtypeStruct((BATCH * SEQ, IN_WIDTH), BF16),
        scratch_shapes=[pltpu.VMEM((SEQ, D_MODEL), BF16),
                        pltpu.VMEM((SEQ, LANES), F32),
                        pltpu.VMEM((SEQ, LANES), F32)],
        compiler_params=pltpu.CompilerParams(dimension_semantics=("arbitrary", "arbitrary"),
                                             vmem_limit_bytes=VMEM_LIMIT),
        name="inproj",
    )(x2d, pos2d, mod4, mod4, norm_pre, inv_freq, w_in_bf16, col_scale)


def _dsw_kernel(q0_ref, k0_ref, v0_ref, q1_ref, k1_ref, v1_ref, q2_ref, k2_ref, v2_ref,
                za_ref, o_ref, qf_scr, kf_scr, vf_scr, acc_scr, num_scr, max_scr, den_scr):
    lane = lax.broadcasted_iota(jnp.int32, (1, LANES), 1)
    lo = lane < HEAD_DIM
    qk_lo = (lane % HEAD_DIM) < HEAD_DIM // 2
    groups = ((q0_ref, k0_ref, v0_ref), (q1_ref, k1_ref, v1_ref), (q2_ref, k2_ref, v2_ref))
    nt = (((1,), (1,)), ((), ()))
    comb_rows = 256

    def geometry(g):
        dil = DSW_DILATIONS[g]
        sub_len = SEQ // dil
        win = min(2 * DSW_CHUNK, sub_len)
        return dil, sub_len, win

    def band_mask(q_start, k_start, win):
        qi = lax.broadcasted_iota(jnp.int32, (DSW_CHUNK, win), 0) + q_start
        kj = lax.broadcasted_iota(jnp.int32, (DSW_CHUNK, win), 1) + k_start
        return jnp.abs(kj - qi) <= DSW_RADIUS

    for g, (q_ref, k_ref, v_ref) in enumerate(groups):
        dil, sub_len, win = geometry(g)
        if dil > 1:
            qf_scr[...] = q_ref[...].astype(F32)
            kf_scr[...] = k_ref[...].astype(F32)
            vf_scr[...] = v_ref[...].astype(F32)
        for res in range(dil):
            for c in range(sub_len // DSW_CHUNK):
                q_start = c * DSW_CHUNK
                k_start = min(max(q_start - DSW_RADIUS, 0), sub_len - win)
                if dil == 1:
                    q = q_ref[q_start:q_start + DSW_CHUNK, :]
                    kw = k_ref[k_start:k_start + win, :]
                    vw = v_ref[k_start:k_start + win, :]
                    rows = pl.ds(q_start, DSW_CHUNK)
                else:
                    rows = pl.ds(res + dil * q_start, DSW_CHUNK, stride=dil)
                    krows = pl.ds(res + dil * k_start, win, stride=dil)
                    q = qf_scr[rows, :].astype(BF16)
                    kw = kf_scr[krows, :].astype(BF16)
                    vw = vf_scr[krows, :].astype(BF16)
                zq = jnp.zeros_like(q)
                q2 = jnp.concatenate([jnp.where(qk_lo, q, zq), jnp.where(qk_lo, zq, q)], axis=0)
                s = lax.dot_general(q2, kw, nt, preferred_element_type=F32)
                band = band_mask(q_start, k_start, win)
                e = jnp.where(jnp.concatenate([band, band], axis=0), jnp.exp2(s), 0.0).astype(BF16)
                one = jnp.ones_like(vw)
                r_a = jnp.dot(e[:DSW_CHUNK], jnp.where(lo, vw, one), preferred_element_type=F32)
                r_b = jnp.dot(e[DSW_CHUNK:], jnp.where(lo, one, vw), preferred_element_type=F32)
                if g == 0:
                    acc_scr[0, rows, :] = r_a
                    acc_scr[1, rows, :] = r_b
                else:
                    acc_scr[0, rows, :] = acc_scr[0, rows, :] + r_a
                    acc_scr[1, rows, :] = acc_scr[1, rows, :] + r_b

    def combine_unshifted(i, carry):
        d_min, d_max = carry
        r = pl.ds(pl.multiple_of(i * comb_rows, comb_rows), comb_rows)
        acc_a, acc_b = acc_scr[0, r, :], acc_scr[1, r, :]
        num = jnp.where(lo, acc_a, acc_b)
        den = pltpu.roll(jnp.where(lo, acc_b, acc_a), HEAD_DIM, 1)
        z = za_ref[r, :].astype(F32)
        o_ref[r, :] = (num / den * (z * _sigmoid(z))).astype(BF16)
        return (jnp.minimum(d_min, jnp.min(den, axis=0, keepdims=True)),
                jnp.maximum(d_max, jnp.max(den, axis=0, keepdims=True)))

    d_min, d_max = lax.fori_loop(
        0, SEQ // comb_rows, combine_unshifted,
        (jnp.full((1, LANES), SAFE_DEN_MAX, F32), jnp.full((1, LANES), SAFE_DEN_MIN, F32)))
    in_range = (jnp.min(d_min) >= SAFE_DEN_MIN) & (jnp.max(d_max) <= SAFE_DEN_MAX)

    @pl.when(jnp.logical_not(in_range))
    def _():
        for g, (q_ref, k_ref, v_ref) in enumerate(groups):
            dil, sub_len, win = geometry(g)
            n_c = sub_len // DSW_CHUNK
            qf_scr[...] = q_ref[...].astype(F32)
            kf_scr[...] = k_ref[...].astype(F32)
            vf_scr[...] = v_ref[...].astype(F32)

            def chunk(t, carry, g=g, dil=dil, sub_len=sub_len, win=win, n_c=n_c):
                res = t // n_c
                q_start = (t % n_c) * DSW_CHUNK
                k_start = jnp.clip(q_start - DSW_RADIUS, 0, sub_len - win)
                rows = pl.ds(res + dil * q_start, DSW_CHUNK, stride=dil)
                krows = pl.ds(res + dil * k_start, win, stride=dil)
                q = qf_scr[rows, :].astype(BF16)
                kw = kf_scr[krows, :].astype(BF16)
                vw = vf_scr[krows, :].astype(BF16)
                zero = jnp.zeros_like(kw)
                band = band_mask(q_start, k_start, win)
                s_a = lax.dot_general(q, jnp.where(qk_lo, kw, zero), nt, preferred_element_type=F32)
                s_b = lax.dot_general(q, jnp.where(qk_lo, zero, kw), nt, preferred_element_type=F32)
                s_a = jnp.where(band, s_a, NEG_INF)
                s_b = jnp.where(band, s_b, NEG_INF)
                m_a = jnp.max(s_a, axis=-1, keepdims=True)
                m_b = jnp.max(s_b, axis=-1, keepdims=True)
                e_a = jnp.exp2(s_a - m_a)
                e_b = jnp.exp2(s_b - m_b)
                d_a = jnp.sum(e_a, axis=-1, keepdims=True)
                d_b = jnp.sum(e_b, axis=-1, keepdims=True)
                num = (jnp.dot(e_a.astype(BF16), jnp.where(lo, vw, zero), preferred_element_type=F32)
                       + jnp.dot(e_b.astype(BF16), jnp.where(lo, zero, vw), preferred_element_type=F32))
                num_scr[g, rows, :] = num
                max_scr[g, rows, :] = jnp.where(lo, m_a, m_b)
                den_scr[g, rows, :] = jnp.where(lo, d_a, d_b)
                return carry

            lax.fori_loop(0, dil * n_c, chunk, 0)

        def combine(i, carry):
            r = pl.ds(pl.multiple_of(i * comb_rows, comb_rows), comb_rows)
            m0, m1, m2 = max_scr[0, r, :], max_scr[1, r, :], max_scr[2, r, :]
            m = jnp.maximum(jnp.maximum(m0, m1), m2)
            w0, w1, w2 = jnp.exp2(m0 - m), jnp.exp2(m1 - m), jnp.exp2(m2 - m)
            num = w0 * num_scr[0, r, :] + w1 * num_scr[1, r, :] + w2 * num_scr[2, r, :]
            den = w0 * den_scr[0, r, :] + w1 * den_scr[1, r, :] + w2 * den_scr[2, r, :]
            z = za_ref[r, :].astype(F32)
            o_ref[r, :] = (num / den * (z * _sigmoid(z))).astype(BF16)
            return carry

        lax.fori_loop(0, SEQ // comb_rows, combine, 0)


def _dsw_attention(proj):
    def col(base, g):
        return pl.BlockSpec((SEQ, LANES), lambda b, hp: (b, base + 4 * g + hp))

    in_specs = []
    for g in range(3):
        in_specs += [col(QA_B, g), col(KA_B, g), col(VA_B, g)]
    in_specs.append(pl.BlockSpec((SEQ, LANES), lambda b, hp: (b, ZA_B + hp)))
    return pl.pallas_call(
        _dsw_kernel,
        grid=(BATCH, DSW_HEADS_PER_GROUP // 2),
        in_specs=in_specs,
        out_specs=pl.BlockSpec((SEQ, LANES), lambda b, hp: (b, hp)),
        out_shape=jax.ShapeDtypeStruct((BATCH * SEQ, DSW_HEADS_PER_GROUP * HEAD_DIM), BF16),
        scratch_shapes=[pltpu.VMEM((SEQ, LANES), F32)] * 3 + [pltpu.VMEM((2, SEQ, LANES), F32)]
        + [pltpu.VMEM((3, SEQ, LANES), F32)] * 3,
        compiler_params=pltpu.CompilerParams(dimension_semantics=("arbitrary", "arbitrary"),
                                             vmem_limit_bytes=VMEM_LIMIT),
        name="dsw_attn",
    )(*([proj] * 10))


def _diff_kernel(q_ref, k_ref, v_ref, z_ref, lq1_ref, lk1_ref, lq2_ref, lk2_ref, subln_ref, o_ref,
                 vt_scr, e_scr, d_scr):
    lane = lax.broadcasted_iota(jnp.int32, (1, LANES), 1)
    lo = (lane % HEAD_DIM) < HEAD_DIM // 2
    vt_scr[...] = v_ref[...].T
    lam = (jnp.exp(jnp.sum(lq1_ref[...] * lk1_ref[...], axis=-1, keepdims=True))
           - jnp.exp(jnp.sum(lq2_ref[...] * lk2_ref[...], axis=-1, keepdims=True)) + LAMBDA_INIT)
    nt = (((1,), (1,)), ((), ()))
    n_kt = SEQ // DIFF_TK
    n_chunks = SEQ // DIFF_TQ
    sublanes = d_scr.shape[2]

    def rows_of(c):
        start = c * DIFF_TQ
        return pl.ds(start if isinstance(c, int) else pl.multiple_of(start, DIFF_TQ), DIFF_TQ)

    def masked_q(c):
        q = q_ref[rows_of(c), :]
        zero = jnp.zeros_like(q)
        return jnp.where(lo, q, zero), jnp.where(lo, zero, q)

    def sublane_partial_sum(e, acc):
        for r in range(e.shape[0] // sublanes):
            part = e[r * sublanes:(r + 1) * sublanes, :]
            acc = part if acc is None else acc + part
        return acc

    def weights_unshifted(c, slot):
        for half, q_half in enumerate(masked_q(c)):
            den = None
            for kt in range(n_kt):
                ks = slice(kt * DIFF_TK, (kt + 1) * DIFF_TK)
                e = jnp.exp2(lax.dot_general(k_ref[ks, :], q_half, nt, preferred_element_type=F32))
                e_scr[slot, half, ks, :] = e.astype(BF16)
                den = sublane_partial_sum(e, den)
            d_scr[slot, half] = den

    def weights_max_shifted(c, slot):
        for half, q_half in enumerate(masked_q(c)):
            s = lax.dot_general(k_ref[...], q_half, nt, preferred_element_type=F32)
            e = jnp.exp2(s - jnp.max(s, axis=0, keepdims=True))
            e_scr[slot, half] = e.astype(BF16)
            d_scr[slot, half] = sublane_partial_sum(e, None)

    def normalise_pv(c, slot):
        den_a = jnp.sum(d_scr[slot, 0], axis=0, keepdims=True)
        den_b = jnp.sum(d_scr[slot, 1], axis=0, keepdims=True)
        inv_a = (1.0 / den_a).astype(BF16)
        inv_b = (lam / den_b).astype(BF16)
        ot = None
        for kt in range(n_kt):
            ks = slice(kt * DIFF_TK, (kt + 1) * DIFF_TK)
            a = e_scr[slot, 0, ks, :] * inv_a - e_scr[slot, 1, ks, :] * inv_b
            pv = jnp.dot(vt_scr[:, ks], a, preferred_element_type=F32)
            ot = pv if ot is None else ot + pv
        o = ot.T
        y = o * lax.rsqrt(jnp.mean(o * o, axis=-1, keepdims=True) + EPS) * subln_ref[...]
        y = y * (1.0 - LAMBDA_INIT)
        z = z_ref[rows_of(c), :].astype(F32)
        o_ref[rows_of(c), :] = (y * (z * _sigmoid(z))).astype(BF16)
        return (jnp.min(jnp.minimum(den_a, den_b), axis=1, keepdims=True),
                jnp.max(jnp.maximum(den_a, den_b), axis=1, keepdims=True))

    weights_unshifted(0, 0)

    def chunk_pair(c0, carry, has_next):
        weights_unshifted(c0 + 1, 1)
        lo0, hi0 = normalise_pv(c0, 0)
        if has_next:
            weights_unshifted(c0 + 2, 0)
        lo1, hi1 = normalise_pv(c0 + 1, 1)
        return (jnp.minimum(carry[0], jnp.minimum(lo0, lo1)), jnp.maximum(carry[1], jnp.maximum(hi0, hi1)))

    carry = (jnp.full((1, 1), SAFE_DEN_MAX, F32), jnp.full((1, 1), SAFE_DEN_MIN, F32))
    for c0 in range(0, n_chunks, 2):
        carry = chunk_pair(c0, carry, c0 + 2 < n_chunks)
    d_min, d_max = carry
    in_range = (jnp.min(d_min) >= SAFE_DEN_MIN) & (jnp.max(d_max) <= SAFE_DEN_MAX)

    @pl.when(jnp.logical_not(in_range))
    def _():
        def body(c, carry):
            weights_max_shifted(c, 0)
            normalise_pv(c, 0)
            return carry

        lax.fori_loop(0, n_chunks, body, 0)


def _diff_attention(proj, lq1, lk1, lq2, lk2, subln):
    def col(base):
        return pl.BlockSpec((SEQ, LANES), lambda b, h: (b, base + h))

    small = pl.BlockSpec((1, HEAD_DIM), lambda b, h: (0, 0))
    return pl.pallas_call(
        _diff_kernel,
        grid=(BATCH, DIFF_HEADS),
        in_specs=[col(QD_B), col(KD_B), col(VD_B), col(ZD_B), small, small, small, small,
                  pl.BlockSpec((1, LANES), lambda b, h: (0, 0))],
        out_specs=pl.BlockSpec((SEQ, LANES), lambda b, h: (b, h)),
        out_shape=jax.ShapeDtypeStruct((BATCH * SEQ, DIFF_HEADS * LANES), BF16),
        scratch_shapes=[pltpu.VMEM((LANES, SEQ), BF16),
                        pltpu.VMEM((2, 2, SEQ, DIFF_TQ), BF16),
                        pltpu.VMEM((2, 2, SUBLANES, DIFF_TQ), F32)],
        compiler_params=pltpu.CompilerParams(dimension_semantics=("arbitrary", "arbitrary"),
                                             vmem_limit_bytes=VMEM_LIMIT),
        name="diff_attn",
    )(proj, proj, proj, proj, lq1, lk1, lq2, lk2, subln)


def _out_kernel(ya_ref, yd_ref, ga_ref, gb_ref, x_ref, gate_ref, wpa_ref, wpb_ref, wout_ref, npost_ref,
                o_ref):
    pa = jnp.dot(ya_ref[...], wpa_ref[...], preferred_element_type=F32)
    pb = jnp.dot(yd_ref[...], wpb_ref[...], preferred_element_type=F32)
    mix = _sigmoid(ga_ref[...].astype(F32)) * pa + _sigmoid(gb_ref[...].astype(F32)) * pb
    y = jnp.dot(mix.astype(BF16), wout_ref[...], preferred_element_type=F32)
    y = y * lax.rsqrt(jnp.mean(y * y, axis=-1, keepdims=True) + EPS) * npost_ref[...]
    o_ref[...] = x_ref[...] + gate_ref[...] * y


def _out_stage(ya, yd, proj, x2d, mod4, wpa, wpb, wout, norm_post):
    per_batch = SEQ // OUT_TM
    gm_block = GM_COL // D_MODEL
    const = lambda i: (0, 0)
    return pl.pallas_call(
        _out_kernel,
        grid=(BATCH * per_batch,),
        in_specs=[pl.BlockSpec((OUT_TM, DSW_HEADS_PER_GROUP * HEAD_DIM), lambda i: (i, 0)),
                  pl.BlockSpec((OUT_TM, D_MODEL), lambda i: (i, 0)),
                  pl.BlockSpec((OUT_TM, D_MODEL), lambda i: (i, gm_block)),
                  pl.BlockSpec((OUT_TM, D_MODEL), lambda i: (i, gm_block + 1)),
                  pl.BlockSpec((OUT_TM, D_MODEL), lambda i: (i, 0)),
                  pl.BlockSpec((None, None, 1, D_MODEL), lambda i: (i // per_batch, 2, 0, 0)),
                  pl.BlockSpec((DSW_HEADS_PER_GROUP * HEAD_DIM, D_MODEL), const),
                  pl.BlockSpec((D_MODEL, D_MODEL), const),
                  pl.BlockSpec((D_MODEL, D_MODEL), const),
                  pl.BlockSpec((1, D_MODEL), const)],
        out_specs=pl.BlockSpec((OUT_TM, D_MODEL), lambda i: (i, 0)),
        out_shape=jax.ShapeDtypeStruct((BATCH * SEQ, D_MODEL), F32),
        compiler_params=pltpu.CompilerParams(dimension_semantics=("arbitrary",),
                                             vmem_limit_bytes=VMEM_LIMIT),
        name="out_stage",
    )(ya, yd, proj, proj, x2d, mod4, wpa, wpb, wout, norm_post)


def kernel(x, c, positions, w_ada, b_ada, norm_pre, w_in, lambda_q1, lambda_k1, lambda_q2, lambda_k2,
           diff_subln, w_proj_a, w_proj_b, w_out, norm_post):
    assert x.shape == (BATCH, SEQ, D_MODEL) and w_in.shape == (1, D_MODEL, IN_WIDTH)
    x2d = x.reshape(BATCH * SEQ, D_MODEL)
    pos2d = jnp.repeat(positions.reshape(BATCH * SEQ // ROPE_PACK, ROPE_PACK), HEAD_DIM // 2, axis=1)
    half = HEAD_DIM // 2
    inv_freq = ROPE_THETA ** (-jnp.arange(half, dtype=F32) / half)
    inv_freq = jnp.tile(inv_freq, LANES // half).reshape(1, LANES)

    mod = _adaln(c, w_ada[0], b_ada)
    mod4 = mod.reshape(BATCH, 3, 1, D_MODEL)
    col = jnp.arange(IN_WIDTH)
    is_q_col = (col < KA_COL) | ((col >= QD_COL) & (col < KD_COL))
    col_scale = jnp.where(is_q_col, Q_SCALE, 1.0).astype(F32).reshape(1, IN_WIDTH)
    proj = _inproj(x2d, pos2d, mod4, norm_pre, inv_freq, _prep_w_in(w_in[0]), col_scale)
    ya = _dsw_attention(proj)
    yd = _diff_attention(proj, lambda_q1, lambda_k1, lambda_q2, lambda_k2, diff_subln)
    out = _out_stage(ya, yd, proj, x2d, mod4, w_proj_a[0].astype(BF16), w_proj_b[0].astype(BF16),
                     w_out[0].astype(BF16), norm_post)
    return out.reshape(BATCH, SEQ, D_MODEL)
```

```python
import math

import jax
import jax.numpy as jnp
from jax import lax
from jax.experimental import pallas as pl
from jax.experimental.pallas import tpu as pltpu

F32 = jnp.float32
BF16 = jnp.bfloat16

D_MODEL = 1024
BATCH = 16
SEQ = 2048
HEAD_DIM = 64
ROPE_THETA = 10000.0
EPS = 1e-6
NEG_INF = -1e30
DSW_DILATIONS = (1, 4, 16)
DSW_RADIUS = 64
DSW_HEADS_PER_GROUP = 8
DIFF_HEADS = 8
LAMBDA_INIT = 0.8 - 0.6 * math.exp(-0.3 * 0)

LANES = 128
SUBLANES = 8
IN_TILE_N = 1024
IN_CHUNK_M = 128
ROPE_PACK = LANES // (HEAD_DIM // 2)
Q_SCALE = HEAD_DIM ** -0.5 * math.log2(math.e)
QA_COL, KA_COL, VA_COL, ZA_COL, QD_COL, KD_COL, VD_COL, ZD_COL, GM_COL, IN_WIDTH = (
    0, 1536, 3072, 4608, 5120, 6144, 7168, 8192, 9216, 11264)
IN_FIRST_PLAIN_TILE = VA_COL // IN_TILE_N
assert VA_COL % IN_TILE_N == 0 and QD_COL % IN_TILE_N == 0 and VD_COL % IN_TILE_N == 0
QA_B, KA_B, VA_B, ZA_B, QD_B, KD_B, VD_B, ZD_B = (
    c // LANES for c in (QA_COL, KA_COL, VA_COL, ZA_COL, QD_COL, KD_COL, VD_COL, ZD_COL))
DSW_CHUNK = 128
DIFF_TQ = 256
DIFF_TK = 256
SAFE_DEN_MIN = 2.0 ** -90
SAFE_DEN_MAX = 2.0 ** 115
OUT_TM = 1024
VMEM_LIMIT = 56 * 1024 * 1024


def _sigmoid(z):
    return 1.0 / (1.0 + jnp.exp(-z))


def _adaln_kernel(c_ref, w_ref, b_ref, o_ref):
    c = c_ref[...]
    sc = c * _sigmoid(c)
    o_ref[...] = jnp.dot(sc, w_ref[...], preferred_element_type=F32,
                         precision=lax.Precision.HIGHEST) + b_ref[...]


def _adaln(c, w_ada, b_ada):
    n_tiles = 3
    return pl.pallas_call(
        _adaln_kernel,
        grid=(n_tiles,),
        in_specs=[pl.BlockSpec((BATCH, D_MODEL), lambda n: (0, 0)),
                  pl.BlockSpec((D_MODEL, D_MODEL), lambda n: (0, n)),
                  pl.BlockSpec((1, D_MODEL), lambda n: (0, n))],
        out_specs=pl.BlockSpec((BATCH, D_MODEL), lambda n: (0, n)),
        out_shape=jax.ShapeDtypeStruct((BATCH, 3 * D_MODEL), F32),
        compiler_params=pltpu.CompilerParams(dimension_semantics=("arbitrary",),
                                             vmem_limit_bytes=VMEM_LIMIT),
        name="adaln",
    )(c, w_ada, b_ada)


def _is_rope_tile(tile):
    return (tile < IN_FIRST_PLAIN_TILE) | ((tile >= QD_COL // IN_TILE_N) & (tile < VD_COL // IN_TILE_N))


def _wprep_kernel(w_ref, o_ref):
    @pl.when(_is_rope_tile(pl.program_id(0)))
    def _():
        group = lax.broadcasted_iota(jnp.int32, (1, LANES), 1) // (HEAD_DIM // 2)
        for j in range(IN_TILE_N // LANES):
            a = w_ref[:, j * LANES:(j + 1) * LANES]
            swapped = jnp.where(group == 1, pltpu.roll(a, LANES - HEAD_DIM // 2, 1),
                                jnp.where(group == 2, pltpu.roll(a, HEAD_DIM // 2, 1), a))
            o_ref[:, j * LANES:(j + 1) * LANES] = swapped.astype(BF16)

    @pl.when(jnp.logical_not(_is_rope_tile(pl.program_id(0))))
    def _():
        o_ref[...] = w_ref[...].astype(BF16)


def _prep_w_in(w_in):
    return pl.pallas_call(
        _wprep_kernel,
        grid=(IN_WIDTH // IN_TILE_N,),
        in_specs=[pl.BlockSpec((D_MODEL, IN_TILE_N), lambda t: (0, t))],
        out_specs=pl.BlockSpec((D_MODEL, IN_TILE_N), lambda t: (0, t)),
        out_shape=jax.ShapeDtypeStruct((D_MODEL, IN_WIDTH), BF16),
        compiler_params=pltpu.CompilerParams(dimension_semantics=("arbitrary",),
                                             vmem_limit_bytes=VMEM_LIMIT),
        name="wprep",
    )(w_in)


def _inproj_tile(n):
    return jnp.where(n < 2, n + IN_FIRST_PLAIN_TILE, jnp.where(n < 2 + IN_FIRST_PLAIN_TILE, n - 2, n))


def _inproj_kernel(x_ref, pos_ref, shift_ref, scale_ref, npre_ref, invf_ref, w_ref, cs_ref, o_ref,
                   h_scr, cos_scr, sin_scr):
    n = pl.program_id(1)
    tile = _inproj_tile(n)
    is_rope = (n >= 2) & _is_rope_tile(tile)

    def chunks(before, epilogue):
        for c in range(SEQ // IN_CHUNK_M):
            rows = slice(c * IN_CHUNK_M, (c + 1) * IN_CHUNK_M)
            before(rows)
            acc = jnp.dot(h_scr[rows, :], w_ref[...], preferred_element_type=F32)
            epilogue(rows, acc)

    def nothing(rows):
        pass

    def plain(rows, acc):
        o_ref[rows, :] = acc.astype(BF16)

    def normalise_rows(rows):
        x = x_ref[rows, :]
        ms = jnp.mean(x * x, axis=-1, keepdims=True)
        y = x * lax.rsqrt(ms + EPS) * npre_ref[...]
        h_scr[rows, :] = (y * (1.0 + scale_ref[...]) + shift_ref[...]).astype(BF16)

    def rotary_tables(rows):
        n_packed = IN_CHUNK_M // ROPE_PACK
        ang = pos_ref[rows.start // ROPE_PACK:rows.start // ROPE_PACK + n_packed, :].astype(F32) * invf_ref[...]
        lane = lax.broadcasted_iota(jnp.int32, (1, LANES), 1)
        sgn = jnp.where(lane < LANES // 2, -1.0, 1.0)
        group = lane // (HEAD_DIM // 2)
        for table, dst, sign in ((jnp.cos(ang), cos_scr, None), (jnp.sin(ang), sin_scr, sgn)):
            for j in range(ROPE_PACK):
                own = jnp.where(group == j, table, 0.0)
                rep = own
                for shift in range(1, ROPE_PACK):
                    rep = rep + pltpu.roll(own, shift * (HEAD_DIM // 2), 1)
                if sign is not None:
                    rep = rep * sign
                dst[pl.ds(rows.start + j, n_packed, stride=ROPE_PACK), :] = rep

    def rope(rows, acc):
        cos, sin = cos_scr[rows, :], sin_scr[rows, :]
        for j in range(IN_TILE_N // LANES):
            a = acc[:, j * LANES:(j + 1) * LANES] * cs_ref[:, j * LANES:(j + 1) * LANES]
            o_ref[rows, j * LANES:(j + 1) * LANES] = (
                a * cos + pltpu.roll(a, LANES // 2, 1) * sin).astype(BF16)

    pl.when(n == 0)(lambda: chunks(normalise_rows, plain))
    pl.when(n == 1)(lambda: chunks(rotary_tables, plain))
    pl.when(is_rope)(lambda: chunks(nothing, rope))
    pl.when((n >= 2) & jnp.logical_not(is_rope))(lambda: chunks(nothing, plain))


def _inproj(x2d, pos2d, mod4, norm_pre, inv_freq, w_in_bf16, col_scale):
    return pl.pallas_call(
        _inproj_kernel,
        grid=(BATCH, IN_WIDTH // IN_TILE_N),
        in_specs=[pl.BlockSpec((SEQ, D_MODEL), lambda b, n: (b, 0)),
                  pl.BlockSpec((SEQ // ROPE_PACK, LANES), lambda b, n: (b, 0)),
                  pl.BlockSpec((None, None, 1, D_MODEL), lambda b, n: (b, 0, 0, 0)),
                  pl.BlockSpec((None, None, 1, D_MODEL), lambda b, n: (b, 1, 0, 0)),
                  pl.BlockSpec((1, D_MODEL), lambda b, n: (0, 0)),
                  pl.BlockSpec((1, LANES), lambda b, n: (0, 0)),
                  pl.BlockSpec((D_MODEL, IN_TILE_N), lambda b, n: (0, _inproj_tile(n))),
                  pl.BlockSpec((1, IN_TILE_N), lambda b, n: (0, _inproj_tile(n)))],
        out_specs=pl.BlockSpec((SEQ, IN_TILE_N), lambda b, n: (b, _inproj_tile(n))),
        out_shape=jax.ShapeDtypeStruct((BATCH * SEQ, IN_WIDTH), BF16),
        scratch_shapes=[pltpu.VMEM((SEQ, D_MODEL), BF16),
                        pltpu.VMEM((SEQ, LANES), F32),
                        pltpu.VMEM((SEQ, LANES), F32)],
        compiler_params=pltpu.CompilerParams(dimension_semantics=("arbitrary", "arbitrary"),
                                             vmem_limit_bytes=VMEM_LIMIT),
        name="inproj",
    )(x2d, pos2d, mod4, mod4, norm_pre, inv_freq, w_in_bf16, col_scale)


def _interleave(main, fill):
    done = 0
    for i, task in enumerate(main):
        task()
        upto = (i + 1) * len(fill) // len(main)
        for extra in fill[done:upto]:
            extra()
        done = upto


def _attn_kernel(qd_ref, kd_ref, vd_ref, zd_ref,
                 q0_ref, k0_ref, v0_ref, q1_ref, k1_ref, v1_ref, q2_ref, k2_ref, v2_ref, za_ref,
                 lq1_ref, lk1_ref, lq2_ref, lk2_ref, subln_ref,
                 yd_ref, ya_ref,
                 vt_scr, e_scr, d_scr, qf_scr, kf_scr, vf_scr, acc_scr, num_scr, max_scr, den_scr):
    lane = lax.broadcasted_iota(jnp.int32, (1, LANES), 1)
    lo = lane < HEAD_DIM
    qk_lo = (lane % HEAD_DIM) < HEAD_DIM // 2
    nt = (((1,), (1,)), ((), ()))

    def all_in_range(stats):
        d_min, d_max = stats[0]
        for lo_i, hi_i in stats[1:]:
            d_min, d_max = jnp.minimum(d_min, lo_i), jnp.maximum(d_max, hi_i)
        return (jnp.min(d_min) >= SAFE_DEN_MIN) & (jnp.max(d_max) <= SAFE_DEN_MAX)

    lam = (jnp.exp(jnp.sum(lq1_ref[...] * lk1_ref[...], axis=-1, keepdims=True))
           - jnp.exp(jnp.sum(lq2_ref[...] * lk2_ref[...], axis=-1, keepdims=True)) + LAMBDA_INIT)
    n_kt = SEQ // DIFF_TK
    n_chunks = SEQ // DIFF_TQ
    sublanes = d_scr.shape[2]

    def rows_of(c):
        start = c * DIFF_TQ
        return pl.ds(start if isinstance(c, int) else pl.multiple_of(start, DIFF_TQ), DIFF_TQ)

    def masked_halves(q):
        zero = jnp.zeros_like(q)
        return jnp.where(qk_lo, q, zero), jnp.where(qk_lo, zero, q)

    def sublane_partial_sum(e, acc):
        for r in range(e.shape[0] // sublanes):
            part = e[r * sublanes:(r + 1) * sublanes, :]
            acc = part if acc is None else acc + part
        return acc

    def weights_unshifted(c, slot):
        tasks = []
        for half in range(2):
            live = {}

            def scores(half=half, live=live):
                q_half = masked_halves(qd_ref[rows_of(c), :])[half]
                live["s"] = lax.dot_general(kd_ref[...], q_half, nt, preferred_element_type=F32)
                live["den"] = None

            def key_tile(kt, half=half, live=live):
                ks = slice(kt * DIFF_TK, (kt + 1) * DIFF_TK)
                e = jnp.exp2(live["s"][ks, :])
                e_scr[slot, half, ks, :] = e.astype(BF16)
                live["den"] = sublane_partial_sum(e, live["den"])

            def sums(half=half, live=live):
                d_scr[slot, half] = live["den"]

            tasks += [scores] + [lambda kt=kt, f=key_tile: f(kt) for kt in range(n_kt)] + [sums]
        return tasks

    def weights_max_shifted(c, slot):
        for half, q_half in enumerate(masked_halves(qd_ref[rows_of(c), :])):
            s = lax.dot_general(kd_ref[...], q_half, nt, preferred_element_type=F32)
            e = jnp.exp2(s - jnp.max(s, axis=0, keepdims=True))
            e_scr[slot, half] = e.astype(BF16)
            d_scr[slot, half] = sublane_partial_sum(e, None)

    def normalise_pv(c, slot, stats):
        live = {}

        def reciprocals():
            live["den_a"] = jnp.sum(d_scr[slot, 0], axis=0, keepdims=True)
            live["den_b"] = jnp.sum(d_scr[slot, 1], axis=0, keepdims=True)
            live["inv_a"] = (1.0 / live["den_a"]).astype(BF16)
            live["inv_b"] = (lam / live["den_b"]).astype(BF16)
            live["ot"] = None

        def key_tile(kt):
            ks = slice(kt * DIFF_TK, (kt + 1) * DIFF_TK)
            a = e_scr[slot, 0, ks, :] * live["inv_a"] - e_scr[slot, 1, ks, :] * live["inv_b"]
            pv = jnp.dot(vt_scr[:, ks], a, preferred_element_type=F32)
            live["ot"] = pv if live["ot"] is None else live["ot"] + pv

        def epilogue():
            o = live["ot"].T
            y = o * lax.rsqrt(jnp.mean(o * o, axis=-1, keepdims=True) + EPS) * subln_ref[...]
            y = y * (1.0 - LAMBDA_INIT)
            z = zd_ref[rows_of(c), :].astype(F32)
            yd_ref[rows_of(c), :] = (y * (z * _sigmoid(z))).astype(BF16)
            stats.append((jnp.min(jnp.minimum(live["den_a"], live["den_b"]), axis=1, keepdims=True),
                          jnp.max(jnp.maximum(live["den_a"], live["den_b"]), axis=1, keepdims=True)))

        return [reciprocals] + [lambda kt=kt: key_tile(kt) for kt in range(n_kt)] + [epilogue]

    def diff_tasks(stats):
        def transpose_values():
            vt_scr[...] = vd_ref[...].T

        tasks = [transpose_values] + weights_unshifted(0, 0)
        for c0 in range(0, n_chunks, 2):
            tasks += weights_unshifted(c0 + 1, 1) + normalise_pv(c0, 0, stats)
            if c0 + 2 < n_chunks:
                tasks += weights_unshifted(c0 + 2, 0)
            tasks += normalise_pv(c0 + 1, 1, stats)
        return tasks

    def diff_fallback(stats):
        @pl.when(jnp.logical_not(all_in_range(stats)))
        def _():
            def body(c, carry):
                weights_max_shifted(c, 0)
                for task in normalise_pv(c, 0, []):
                    task()
                return carry

            lax.fori_loop(0, n_chunks, body, 0)

    groups = ((q0_ref, k0_ref, v0_ref), (q1_ref, k1_ref, v1_ref), (q2_ref, k2_ref, v2_ref))
    comb_rows = 256

    def geometry(g):
        dil = DSW_DILATIONS[g]
        sub_len = SEQ // dil
        win = min(2 * DSW_CHUNK, sub_len)
        return dil, sub_len, win

    def band_mask(q_start, k_start, win):
        qi = lax.broadcasted_iota(jnp.int32, (DSW_CHUNK, win), 0) + q_start
        kj = lax.broadcasted_iota(jnp.int32, (DSW_CHUNK, win), 1) + k_start
        return jnp.abs(kj - qi) <= DSW_RADIUS

    def to_f32_scratch(g):
        q_ref, k_ref, v_ref = groups[g]
        qf_scr[...] = q_ref[...].astype(F32)
        kf_scr[...] = k_ref[...].astype(F32)
        vf_scr[...] = v_ref[...].astype(F32)

    def window_unshifted(g, res, c):
        live = {}
        q_ref, k_ref, v_ref = groups[g]
        dil, sub_len, win = geometry(g)
        q_start = c * DSW_CHUNK
        k_start = min(max(q_start - DSW_RADIUS, 0), sub_len - win)
        if dil == 1:
            rows = pl.ds(q_start, DSW_CHUNK)
        else:
            rows = pl.ds(res + dil * q_start, DSW_CHUNK, stride=dil)
            krows = pl.ds(res + dil * k_start, win, stride=dil)

        def scores():
            if dil == 1:
                q = q_ref[q_start:q_start + DSW_CHUNK, :]
                kw = k_ref[k_start:k_start + win, :]
            else:
                q = qf_scr[rows, :].astype(BF16)
                kw = kf_scr[krows, :].astype(BF16)
            zq = jnp.zeros_like(q)
            q2 = jnp.concatenate([jnp.where(qk_lo, q, zq), jnp.where(qk_lo, zq, q)], axis=0)
            live["s"] = lax.dot_general(q2, kw, nt, preferred_element_type=F32)

        def weights():
            band = band_mask(q_start, k_start, win)
            live["e"] = jnp.where(jnp.concatenate([band, band], axis=0), jnp.exp2(live["s"]), 0.0).astype(BF16)

        def accumulate():
            vw = v_ref[k_start:k_start + win, :] if dil == 1 else vf_scr[krows, :].astype(BF16)
            one = jnp.ones_like(vw)
            r_a = jnp.dot(live["e"][:DSW_CHUNK], jnp.where(lo, vw, one), preferred_element_type=F32)
            r_b = jnp.dot(live["e"][DSW_CHUNK:], jnp.where(lo, one, vw), preferred_element_type=F32)
            if g == 0:
                acc_scr[0, rows, :] = r_a
                acc_scr[1, rows, :] = r_b
            else:
                acc_scr[0, rows, :] = acc_scr[0, rows, :] + r_a
                acc_scr[1, rows, :] = acc_scr[1, rows, :] + r_b

        return [scores, weights, accumulate]

    def combine_unshifted(i):
        r = pl.ds(i * comb_rows, comb_rows)
        acc_a, acc_b = acc_scr[0, r, :], acc_scr[1, r, :]
        num = jnp.where(lo, acc_a, acc_b)
        den = pltpu.roll(jnp.where(lo, acc_b, acc_a), HEAD_DIM, 1)
        z = za_ref[r, :].astype(F32)
        ya_ref[r, :] = (num / den * (z * _sigmoid(z))).astype(BF16)
        return jnp.min(den, axis=0, keepdims=True), jnp.max(den, axis=0, keepdims=True)

    def dsw_tasks(group_ids, stats):
        tasks = []
        for g in group_ids:
            dil, sub_len, _ = geometry(g)
            if dil > 1:
                tasks.append(lambda g=g: to_f32_scratch(g))
            for res in range(dil):
                for c in range(sub_len // DSW_CHUNK):
                    tasks += window_unshifted(g, res, c)
        if stats is not None:
            for i in range(SEQ // comb_rows):
                tasks.append(lambda i=i: stats.append(combine_unshifted(i)))
        return tasks

    def dsw_fallback(stats):
        @pl.when(jnp.logical_not(all_in_range(stats)))
        def _():
            for g, (q_ref, k_ref, v_ref) in enumerate(groups):
                dil, sub_len, win = geometry(g)
                n_c = sub_len // DSW_CHUNK
                to_f32_scratch(g)

                def chunk(t, carry, g=g, dil=dil, sub_len=sub_len, win=win, n_c=n_c):
                    res = t // n_c
                    q_start = (t % n_c) * DSW_CHUNK
                    k_start = jnp.clip(q_start - DSW_RADIUS, 0, sub_len - win)
                    rows = pl.ds(res + dil * q_start, DSW_CHUNK, stride=dil)
                    krows = pl.ds(res + dil * k_start, win, stride=dil)
                    q = qf_scr[rows, :].astype(BF16)
                    kw = kf_scr[krows, :].astype(BF16)
                    vw = vf_scr[krows, :].astype(BF16)
                    zero = jnp.zeros_like(kw)
                    band = band_mask(q_start, k_start, win)
                    s_a = lax.dot_general(q, jnp.where(qk_lo, kw, zero), nt, preferred_element_type=F32)
                    s_b = lax.dot_general(q, jnp.where(qk_lo, zero, kw), nt, preferred_element_type=F32)
                    s_a = jnp.where(band, s_a, NEG_INF)
                    s_b = jnp.where(band, s_b, NEG_INF)
                    m_a = jnp.max(s_a, axis=-1, keepdims=True)
                    m_b = jnp.max(s_b, axis=-1, keepdims=True)
                    e_a = jnp.exp2(s_a - m_a)
                    e_b = jnp.exp2(s_b - m_b)
                    d_a = jnp.sum(e_a, axis=-1, keepdims=True)
                    d_b = jnp.sum(e_b, axis=-1, keepdims=True)
                    num = (jnp.dot(e_a.astype(BF16), jnp.where(lo, vw, zero), preferred_element_type=F32)
                           + jnp.dot(e_b.astype(BF16), jnp.where(lo, zero, vw), preferred_element_type=F32))
                    num_scr[g, rows, :] = num
                    max_scr[g, rows, :] = jnp.where(lo, m_a, m_b)
                    den_scr[g, rows, :] = jnp.where(lo, d_a, d_b)
                    return carry

                lax.fori_loop(0, dil * n_c, chunk, 0)

            def combine(i, carry):
                r = pl.ds(pl.multiple_of(i * comb_rows, comb_rows), comb_rows)
                m0, m1, m2 = max_scr[0, r, :], max_scr[1, r, :], max_scr[2, r, :]
                m = jnp.maximum(jnp.maximum(m0, m1), m2)
                w0, w1, w2 = jnp.exp2(m0 - m), jnp.exp2(m1 - m), jnp.exp2(m2 - m)
                num = w0 * num_scr[0, r, :] + w1 * num_scr[1, r, :] + w2 * num_scr[2, r, :]
                den = w0 * den_scr[0, r, :] + w1 * den_scr[1, r, :] + w2 * den_scr[2, r, :]
                z = za_ref[r, :].astype(F32)
                ya_ref[r, :] = (num / den * (z * _sigmoid(z))).astype(BF16)
                return carry

            lax.fori_loop(0, SEQ // comb_rows, combine, 0)

    first_half_of_pair = pl.program_id(1) % 2 == 0

    @pl.when(first_half_of_pair)
    def _():
        diff_stats = []
        _interleave(diff_tasks(diff_stats), dsw_tasks((0, 1), None))
        diff_fallback(diff_stats)

    @pl.when(jnp.logical_not(first_half_of_pair))
    def _():
        diff_stats, dsw_stats = [], []
        _interleave(diff_tasks(diff_stats), dsw_tasks((2,), dsw_stats))
        diff_fallback(diff_stats)
        dsw_fallback(dsw_stats)


def _attention(proj, lq1, lk1, lq2, lk2, subln):
    def diff_col(base):
        return pl.BlockSpec((SEQ, LANES), lambda b, h: (b, base + h))

    def dsw_col(base, g):
        return pl.BlockSpec((SEQ, LANES), lambda b, h: (b, base + (DSW_HEADS_PER_GROUP // 2) * g + h // 2))

    in_specs = [diff_col(QD_B), diff_col(KD_B), diff_col(VD_B), diff_col(ZD_B)]
    for g in range(len(DSW_DILATIONS)):
        in_specs += [dsw_col(QA_B, g), dsw_col(KA_B, g), dsw_col(VA_B, g)]
    in_specs.append(pl.BlockSpec((SEQ, LANES), lambda b, h: (b, ZA_B + h // 2)))
    small = pl.BlockSpec((1, HEAD_DIM), lambda b, h: (0, 0))
    in_specs += [small, small, small, small, pl.BlockSpec((1, LANES), lambda b, h: (0, 0))]
    assert DIFF_HEADS == 2 * (DSW_HEADS_PER_GROUP // 2)
    return pl.pallas_call(
        _attn_kernel,
        grid=(BATCH, DIFF_HEADS),
        in_specs=in_specs,
        out_specs=[pl.BlockSpec((SEQ, LANES), lambda b, h: (b, h)),
                   pl.BlockSpec((SEQ, LANES), lambda b, h: (b, h // 2))],
        out_shape=[jax.ShapeDtypeStruct((BATCH * SEQ, DIFF_HEADS * LANES), BF16),
                   jax.ShapeDtypeStruct((BATCH * SEQ, DSW_HEADS_PER_GROUP * HEAD_DIM), BF16)],
        scratch_shapes=[pltpu.VMEM((LANES, SEQ), BF16),
                        pltpu.VMEM((2, 2, SEQ, DIFF_TQ), BF16),
                        pltpu.VMEM((2, 2, SUBLANES, DIFF_TQ), F32)]
        + [pltpu.VMEM((SEQ, LANES), F32)] * 3
        + [pltpu.VMEM((2, SEQ, LANES), F32)]
        + [pltpu.VMEM((3, SEQ, LANES), F32)] * 3,
        compiler_params=pltpu.CompilerParams(dimension_semantics=("arbitrary", "arbitrary"),
                                             vmem_limit_bytes=VMEM_LIMIT),
        name="attention",
    )(*([proj] * 14), lq1, lk1, lq2, lk2, subln)


def _out_kernel(ya_ref, yd_ref, ga_ref, gb_ref, x_ref, gate_ref, wpa_ref, wpb_ref, wout_ref, npost_ref,
                o_ref):
    pa = jnp.dot(ya_ref[...], wpa_ref[...], preferred_element_type=F32)
    pb = jnp.dot(yd_ref[...], wpb_ref[...], preferred_element_type=F32)
    mix = _sigmoid(ga_ref[...].astype(F32)) * pa + _sigmoid(gb_ref[...].astype(F32)) * pb
    y = jnp.dot(mix.astype(BF16), wout_ref[...], preferred_element_type=F32)
    y = y * lax.rsqrt(jnp.mean(y * y, axis=-1, keepdims=True) + EPS) * npost_ref[...]
    o_ref[...] = x_ref[...] + gate_ref[...] * y


def _out_stage(ya, yd, proj, x2d, mod4, wpa, wpb, wout, norm_post):
    per_batch = SEQ // OUT_TM
    gm_block = GM_COL // D_MODEL
    const = lambda i: (0, 0)
    return pl.pallas_call(
        _out_kernel,
        grid=(BATCH * per_batch,),
        in_specs=[pl.BlockSpec((OUT_TM, DSW_HEADS_PER_GROUP * HEAD_DIM), lambda i: (i, 0)),
                  pl.BlockSpec((OUT_TM, D_MODEL), lambda i: (i, 0)),
                  pl.BlockSpec((OUT_TM, D_MODEL), lambda i: (i, gm_block)),
                  pl.BlockSpec((OUT_TM, D_MODEL), lambda i: (i, gm_block + 1)),
                  pl.BlockSpec((OUT_TM, D_MODEL), lambda i: (i, 0)),
                  pl.BlockSpec((None, None, 1, D_MODEL), lambda i: (i // per_batch, 2, 0, 0)),
                  pl.BlockSpec((DSW_HEADS_PER_GROUP * HEAD_DIM, D_MODEL), const),
                  pl.BlockSpec((D_MODEL, D_MODEL), const),
                  pl.BlockSpec((D_MODEL, D_MODEL), const),
                  pl.BlockSpec((1, D_MODEL), const)],
        out_specs=pl.BlockSpec((OUT_TM, D_MODEL), lambda i: (i, 0)),
        out_shape=jax.ShapeDtypeStruct((BATCH * SEQ, D_MODEL), F32),
        compiler_params=pltpu.CompilerParams(dimension_semantics=("arbitrary",),
                                             vmem_limit_bytes=VMEM_LIMIT),
        name="out_stage",
    )(ya, yd, proj, proj, x2d, mod4, wpa, wpb, wout, norm_post)


def kernel(x, c, positions, w_ada, b_ada, norm_pre, w_in, lambda_q1, lambda_k1, lambda_q2, lambda_k2,
           diff_subln, w_proj_a, w_proj_b, w_out, norm_post):
    assert x.shape == (BATCH, SEQ, D_MODEL) and w_in.shape == (1, D_MODEL, IN_WIDTH)
    x2d = x.reshape(BATCH * SEQ, D_MODEL)
    pos2d = jnp.repeat(positions.reshape(BATCH * SEQ // ROPE_PACK, ROPE_PACK), HEAD_DIM // 2, axis=1)
    half = HEAD_DIM // 2
    inv_freq = ROPE_THETA ** (-jnp.arange(half, dtype=F32) / half)
    inv_freq = jnp.tile(inv_freq, LANES // half).reshape(1, LANES)

    mod = _adaln(c, w_ada[0], b_ada)
    mod4 = mod.reshape(BATCH, 3, 1, D_MODEL)
    col = jnp.arange(IN_WIDTH)
    is_q_col = (col < KA_COL) | ((col >= QD_COL) & (col < KD_COL))
    col_scale = jnp.where(is_q_col, Q_SCALE, 1.0).astype(F32).reshape(1, IN_WIDTH)
    proj = _inproj(x2d, pos2d, mod4, norm_pre, inv_freq, _prep_w_in(w_in[0]), col_scale)
    yd, ya = _attention(proj, lambda_q1, lambda_k1, lambda_q2, lambda_k2, diff_subln)
    out = _out_stage(ya, yd, proj, x2d, mod4, w_proj_a[0].astype(BF16), w_proj_b[0].astype(BF16),
                     w_out[0].astype(BF16), norm_post)
    return out.reshape(BATCH, SEQ, D_MODEL)
```

```python
import math

import jax
import jax.numpy as jnp
from jax import lax
from jax.experimental import pallas as pl
from jax.experimental.pallas import tpu as pltpu

F32 = jnp.float32
BF16 = jnp.bfloat16

D_MODEL = 1024
BATCH = 16
SEQ = 2048
HEAD_DIM = 64
ROPE_THETA = 10000.0
EPS = 1e-6
NEG_INF = -1e30
DSW_DILATIONS = (1, 4, 16)
DSW_RADIUS = 64
DSW_HEADS_PER_GROUP = 8
DIFF_HEADS = 8
LAMBDA_INIT = 0.8 - 0.6 * math.exp(-0.3 * 0)

LANES = 128
SUBLANES = 8
IN_TILE_N = 1024
IN_CHUNK_M = 128
ROPE_PACK = LANES // (HEAD_DIM // 2)
Q_SCALE = HEAD_DIM ** -0.5 * math.log2(math.e)
QA_COL, KA_COL, VA_COL, ZA_COL, QD_COL, KD_COL, VD_COL, ZD_COL, GM_COL, IN_WIDTH = (
    0, 1536, 3072, 4608, 5120, 6144, 7168, 8192, 9216, 11264)
IN_FIRST_PLAIN_TILE = VA_COL // IN_TILE_N
assert VA_COL % IN_TILE_N == 0 and QD_COL % IN_TILE_N == 0 and VD_COL % IN_TILE_N == 0
QA_B, KA_B, VA_B, ZA_B, QD_B, KD_B, VD_B, ZD_B = (
    c // LANES for c in (QA_COL, KA_COL, VA_COL, ZA_COL, QD_COL, KD_COL, VD_COL, ZD_COL))
DSW_CHUNK = 128
DIFF_TQ = 256
DIFF_TK = 256
SAFE_DEN_MIN = 2.0 ** -90
SAFE_DEN_MAX = 2.0 ** 115
OUT_TM = 1024
VMEM_LIMIT = 56 * 1024 * 1024


def _sigmoid(z):
    return 1.0 / (1.0 + jnp.exp(-z))


def _adaln_kernel(c_ref, w_ref, b_ref, o_ref):
    c = c_ref[...]
    sc = c * _sigmoid(c)
    o_ref[...] = jnp.dot(sc, w_ref[...], preferred_element_type=F32,
                         precision=lax.Precision.HIGHEST) + b_ref[...]


def _adaln(c, w_ada, b_ada):
    n_tiles = 3
    return pl.pallas_call(
        _adaln_kernel,
        grid=(n_tiles,),
        in_specs=[pl.BlockSpec((BATCH, D_MODEL), lambda n: (0, 0)),
                  pl.BlockSpec((D_MODEL, D_MODEL), lambda n: (0, n)),
                  pl.BlockSpec((1, D_MODEL), lambda n: (0, n))],
        out_specs=pl.BlockSpec((BATCH, D_MODEL), lambda n: (0, n)),
        out_shape=jax.ShapeDtypeStruct((BATCH, 3 * D_MODEL), F32),
        compiler_params=pltpu.CompilerParams(dimension_semantics=("arbitrary",),
                                             vmem_limit_bytes=VMEM_LIMIT),
        name="adaln",
    )(c, w_ada, b_ada)


def _is_rope_tile(tile):
    return (tile < IN_FIRST_PLAIN_TILE) | ((tile >= QD_COL // IN_TILE_N) & (tile < VD_COL // IN_TILE_N))


def _wprep_kernel(w_ref, o_ref):
    @pl.when(_is_rope_tile(pl.program_id(0)))
    def _():
        group = lax.broadcasted_iota(jnp.int32, (1, LANES), 1) // (HEAD_DIM // 2)
        for j in range(IN_TILE_N // LANES):
            a = w_ref[:, j * LANES:(j + 1) * LANES]
            swapped = jnp.where(group == 1, pltpu.roll(a, LANES - HEAD_DIM // 2, 1),
                                jnp.where(group == 2, pltpu.roll(a, HEAD_DIM // 2, 1), a))
            o_ref[:, j * LANES:(j + 1) * LANES] = swapped.astype(BF16)

    @pl.when(jnp.logical_not(_is_rope_tile(pl.program_id(0))))
    def _():
        o_ref[...] = w_ref[...].astype(BF16)


def _prep_w_in(w_in):
    return pl.pallas_call(
        _wprep_kernel,
        grid=(IN_WIDTH // IN_TILE_N,),
        in_specs=[pl.BlockSpec((D_MODEL, IN_TILE_N), lambda t: (0, t))],
        out_specs=pl.BlockSpec((D_MODEL, IN_TILE_N), lambda t: (0, t)),
        out_shape=jax.ShapeDtypeStruct((D_MODEL, IN_WIDTH), BF16),
        compiler_params=pltpu.CompilerParams(dimension_semantics=("arbitrary",),
                                             vmem_limit_bytes=VMEM_LIMIT),
        name="wprep",
    )(w_in)


def _inproj_tile(n):
    return jnp.where(n < 2, n + IN_FIRST_PLAIN_TILE, jnp.where(n < 2 + IN_FIRST_PLAIN_TILE, n - 2, n))


def _inproj_kernel(x_ref, pos_ref, shift_ref, scale_ref, npre_ref, invf_ref, w_ref, cs_ref, o_ref,
                   h_scr, cos_scr, sin_scr):
    n = pl.program_id(1)
    tile = _inproj_tile(n)
    is_rope = (n >= 2) & _is_rope_tile(tile)

    def chunks(before, epilogue):
        for c in range(SEQ // IN_CHUNK_M):
            rows = slice(c * IN_CHUNK_M, (c + 1) * IN_CHUNK_M)
            before(rows)
            acc = jnp.dot(h_scr[rows, :], w_ref[...], preferred_element_type=F32)
            epilogue(rows, acc)

    def nothing(rows):
        pass

    def plain(rows, acc):
        o_ref[rows, :] = acc.astype(BF16)

    def normalise_rows(rows):
        x = x_ref[rows, :]
        ms = jnp.mean(x * x, axis=-1, keepdims=True)
        y = x * lax.rsqrt(ms + EPS) * npre_ref[...]
        h_scr[rows, :] = (y * (1.0 + scale_ref[...]) + shift_ref[...]).astype(BF16)

    def rotary_tables(rows):
        n_packed = IN_CHUNK_M // ROPE_PACK
        ang = pos_ref[rows.start // ROPE_PACK:rows.start // ROPE_PACK + n_packed, :].astype(F32) * invf_ref[...]
        lane = lax.broadcasted_iota(jnp.int32, (1, LANES), 1)
        sgn = jnp.where(lane < LANES // 2, -1.0, 1.0)
        group = lane // (HEAD_DIM // 2)
        for table, dst, sign in ((jnp.cos(ang), cos_scr, None), (jnp.sin(ang), sin_scr, sgn)):
            for j in range(ROPE_PACK):
                own = jnp.where(group == j, table, 0.0)
                rep = own
                for shift in range(1, ROPE_PACK):
                    rep = rep + pltpu.roll(own, shift * (HEAD_DIM // 2), 1)
                if sign is not None:
                    rep = rep * sign
                dst[pl.ds(rows.start + j, n_packed, stride=ROPE_PACK), :] = rep

    def rope(rows, acc):
        cos, sin = cos_scr[rows, :], sin_scr[rows, :]
        for j in range(IN_TILE_N // LANES):
            a = acc[:, j * LANES:(j + 1) * LANES] * cs_ref[:, j * LANES:(j + 1) * LANES]
            o_ref[rows, j * LANES:(j + 1) * LANES] = (
                a * cos + pltpu.roll(a, LANES // 2, 1) * sin).astype(BF16)

    pl.when(n == 0)(lambda: chunks(normalise_rows, plain))
    pl.when(n == 1)(lambda: chunks(rotary_tables, plain))
    pl.when(is_rope)(lambda: chunks(nothing, rope))
    pl.when((n >= 2) & jnp.logical_not(is_rope))(lambda: chunks(nothing, plain))


def _inproj(x2d, pos2d, mod4, norm_pre, inv_freq, w_in_bf16, col_scale):
    return pl.pallas_call(
        _inproj_kernel,
        grid=(BATCH, IN_WIDTH // IN_TILE_N),
        in_specs=[pl.BlockSpec((SEQ, D_MODEL), lambda b, n: (b, 0)),
                  pl.BlockSpec((SEQ // ROPE_PACK, LANES), lambda b, n: (b, 0)),
                  pl.BlockSpec((None, None, 1, D_MODEL), lambda b, n: (b, 0, 0, 0)),
                  pl.BlockSpec((None, None, 1, D_MODEL), lambda b, n: (b, 1, 0, 0)),
                  pl.BlockSpec((1, D_MODEL), lambda b, n: (0, 0)),
                  pl.BlockSpec((1, LANES), lambda b, n: (0, 0)),
                  pl.BlockSpec((D_MODEL, IN_TILE_N), lambda b, n: (0, _inproj_tile(n))),
                  pl.BlockSpec((1, IN_TILE_N), lambda b, n: (0, _inproj_tile(n)))],
        out_specs=pl.BlockSpec((SEQ, IN_TILE_N), lambda b, n: (b, _inproj_tile(n))),
        out_shape=jax.ShapeDtypeStruct((BATCH * SEQ, IN_WIDTH), BF16),
        scratch_shapes=[pltpu.VMEM((SEQ, D_MODEL), BF16),
                        pltpu.VMEM((SEQ, LANES), F32),
                        pltpu.VMEM((SEQ, LANES), F32)],
        compiler_params=pltpu.CompilerParams(dimension_semantics=("arbitrary", "arbitrary"),
                                             vmem_limit_bytes=VMEM_LIMIT),
        name="inproj",
    )(x2d, pos2d, mod4, mod4, norm_pre, inv_freq, w_in_bf16, col_scale)


def _dsw_kernel(q0_ref, k0_ref, v0_ref, q1_ref, k1_ref, v1_ref, q2_ref, k2_ref, v2_ref,
                za_ref, o_ref, qf_scr, kf_scr, vf_scr, acc_scr, num_scr, max_scr, den_scr):
    lane = lax.broadcasted_iota(jnp.int32, (1, LANES), 1)
    lo = lane < HEAD_DIM
    qk_lo = (lane % HEAD_DIM) < HEAD_DIM // 2
    groups = ((q0_ref, k0_ref, v0_ref), (q1_ref, k1_ref, v1_ref), (q2_ref, k2_ref, v2_ref))
    nt = (((1,), (1,)), ((), ()))
    comb_rows = 256

    def geometry(g):
        dil = DSW_DILATIONS[g]
        sub_len = SEQ // dil
        win = min(2 * DSW_CHUNK, sub_len)
        return dil, sub_len, win

    def band_mask(q_start, k_start, win):
        qi = lax.broadcasted_iota(jnp.int32, (DSW_CHUNK, win), 0) + q_start
        kj = lax.broadcasted_iota(jnp.int32, (DSW_CHUNK, win), 1) + k_start
        return jnp.abs(kj - qi) <= DSW_RADIUS

    for g, (q_ref, k_ref, v_ref) in enumerate(groups):
        dil, sub_len, win = geometry(g)
        if dil > 1:
            qf_scr[...] = q_ref[...].astype(F32)
            kf_scr[...] = k_ref[...].astype(F32)
            vf_scr[...] = v_ref[...].astype(F32)
        for res in range(dil):
            for c in range(sub_len // DSW_CHUNK):
                q_start = c * DSW_CHUNK
                k_start = min(max(q_start - DSW_RADIUS, 0), sub_len - win)
                if dil == 1:
                    q = q_ref[q_start:q_start + DSW_CHUNK, :]
                    kw = k_ref[k_start:k_start + win, :]
                    vw = v_ref[k_start:k_start + win, :]
                    rows = pl.ds(q_start, DSW_CHUNK)
                else:
                    rows = pl.ds(res + dil * q_start, DSW_CHUNK, stride=dil)
                    krows = pl.ds(res + dil * k_start, win, stride=dil)
                    q = qf_scr[rows, :].astype(BF16)
                    kw = kf_scr[krows, :].astype(BF16)
                    vw = vf_scr[krows, :].astype(BF16)
                zq = jnp.zeros_like(q)
                q2 = jnp.concatenate([jnp.where(qk_lo, q, zq), jnp.where(qk_lo, zq, q)], axis=0)
                s = lax.dot_general(q2, kw, nt, preferred_element_type=F32)
                band = band_mask(q_start, k_start, win)
                e = jnp.where(jnp.concatenate([band, band], axis=0), jnp.exp2(s), 0.0).astype(BF16)
                one = jnp.ones_like(vw)
                r_a = jnp.dot(e[:DSW_CHUNK], jnp.where(lo, vw, one), preferred_element_type=F32)
                r_b = jnp.dot(e[DSW_CHUNK:], jnp.where(lo, one, vw), preferred_element_type=F32)
                if g == 0:
                    acc_scr[0, rows, :] = r_a
                    acc_scr[1, rows, :] = r_b
                else:
                    acc_scr[0, rows, :] = acc_scr[0, rows, :] + r_a
                    acc_scr[1, rows, :] = acc_scr[1, rows, :] + r_b

    def combine_unshifted(i, carry):
        d_min, d_max = carry
        r = pl.ds(pl.multiple_of(i * comb_rows, comb_rows), comb_rows)
        acc_a, acc_b = acc_scr[0, r, :], acc_scr[1, r, :]
        num = jnp.where(lo, acc_a, acc_b)
        den = pltpu.roll(jnp.where(lo, acc_b, acc_a), HEAD_DIM, 1)
        z = za_ref[r, :].astype(F32)
        o_ref[r, :] = (num / den * (z * _sigmoid(z))).astype(BF16)
        return (jnp.minimum(d_min, jnp.min(den, axis=0, keepdims=True)),
                jnp.maximum(d_max, jnp.max(den, axis=0, keepdims=True)))

    d_min, d_max = lax.fori_loop(
        0, SEQ // comb_rows, combine_unshifted,
        (jnp.full((1, LANES), SAFE_DEN_MAX, F32), jnp.full((1, LANES), SAFE_DEN_MIN, F32)))
    in_range = (jnp.min(d_min) >= SAFE_DEN_MIN) & (jnp.max(d_max) <= SAFE_DEN_MAX)

    @pl.when(jnp.logical_not(in_range))
    def _():
        for g, (q_ref, k_ref, v_ref) in enumerate(groups):
            dil, sub_len, win = geometry(g)
            n_c = sub_len // DSW_CHUNK
            qf_scr[...] = q_ref[...].astype(F32)
            kf_scr[...] = k_ref[...].astype(F32)
            vf_scr[...] = v_ref[...].astype(F32)

            def chunk(t, carry, g=g, dil=dil, sub_len=sub_len, win=win, n_c=n_c):
                res = t // n_c
                q_start = (t % n_c) * DSW_CHUNK
                k_start = jnp.clip(q_start - DSW_RADIUS, 0, sub_len - win)
                rows = pl.ds(res + dil * q_start, DSW_CHUNK, stride=dil)
                krows = pl.ds(res + dil * k_start, win, stride=dil)
                q = qf_scr[rows, :].astype(BF16)
                kw = kf_scr[krows, :].astype(BF16)
                vw = vf_scr[krows, :].astype(BF16)
                zero = jnp.zeros_like(kw)
                band = band_mask(q_start, k_start, win)
                s_a = lax.dot_general(q, jnp.where(qk_lo, kw, zero), nt, preferred_element_type=F32)
                s_b = lax.dot_general(q, jnp.where(qk_lo, zero, kw), nt, preferred_element_type=F32)
                s_a = jnp.where(band, s_a, NEG_INF)
                s_b = jnp.where(band, s_b, NEG_INF)
                m_a = jnp.max(s_a, axis=-1, keepdims=True)
                m_b = jnp.max(s_b, axis=-1, keepdims=True)
                e_a = jnp.exp2(s_a - m_a)
                e_b = jnp.exp2(s_b - m_b)
                d_a = jnp.sum(e_a, axis=-1, keepdims=True)
                d_b = jnp.sum(e_b, axis=-1, keepdims=True)
                num = (jnp.dot(e_a.astype(BF16), jnp.where(lo, vw, zero), preferred_element_type=F32)
                       + jnp.dot(e_b.astype(BF16), jnp.where(lo, zero, vw), preferred_element_type=F32))
                num_scr[g, rows, :] = num
                max_scr[g, rows, :] = jnp.where(lo, m_a, m_b)
                den_scr[g, rows, :] = jnp.where(lo, d_a, d_b)
                return carry

            lax.fori_loop(0, dil * n_c, chunk, 0)

        def combine(i, carry):
            r = pl.ds(pl.multiple_of(i * comb_rows, comb_rows), comb_rows)
            m0, m1, m2 = max_scr[0, r, :], max_scr[1, r, :], max_scr[2, r, :]
            m = jnp.maximum(jnp.maximum(m0, m1), m2)
            w0, w1, w2 = jnp.exp2(m0 - m), jnp.exp2(m1 - m), jnp.exp2(m2 - m)
            num = w0 * num_scr[0, r, :] + w1 * num_scr[1, r, :] + w2 * num_scr[2, r, :]
            den = w0 * den_scr[0, r, :] + w1 * den_scr[1, r, :] + w2 * den_scr[2, r, :]
            z = za_ref[r, :].astype(F32)
            o_ref[r, :] = (num / den * (z * _sigmoid(z))).astype(BF16)
            return carry

        lax.fori_loop(0, SEQ // comb_rows, combine, 0)


def _dsw_attention(proj):
    def col(base, g):
        return pl.BlockSpec((SEQ, LANES), lambda b, hp: (b, base + 4 * g + hp))

    in_specs = []
    for g in range(3):
        in_specs += [col(QA_B, g), col(KA_B, g), col(VA_B, g)]
    in_specs.append(pl.BlockSpec((SEQ, LANES), lambda b, hp: (b, ZA_B + hp)))
    return pl.pallas_call(
        _dsw_kernel,
        grid=(BATCH, DSW_HEADS_PER_GROUP // 2),
        in_specs=in_specs,
        out_specs=pl.BlockSpec((SEQ, LANES), lambda b, hp: (b, hp)),
        out_shape=jax.ShapeDtypeStruct((BATCH * SEQ, DSW_HEADS_PER_GROUP * HEAD_DIM), BF16),
        scratch_shapes=[pltpu.VMEM((SEQ, LANES), F32)] * 3 + [pltpu.VMEM((2, SEQ, LANES), F32)]
        + [pltpu.VMEM((3, SEQ, LANES), F32)] * 3,
        compiler_params=pltpu.CompilerParams(dimension_semantics=("arbitrary", "arbitrary"),
                                             vmem_limit_bytes=VMEM_LIMIT),
        name="dsw_attn",
    )(*([proj] * 10))


def _diff_kernel(q_ref, k_ref, v_ref, z_ref, lq1_ref, lk1_ref, lq2_ref, lk2_ref, subln_ref, o_ref,
                 vt_scr, e_scr, d_scr):
    lane = lax.broadcasted_iota(jnp.int32, (1, LANES), 1)
    lo = (lane % HEAD_DIM) < HEAD_DIM // 2
    vt_scr[...] = v_ref[...].T
    lam = (jnp.exp(jnp.sum(lq1_ref[...] * lk1_ref[...], axis=-1, keepdims=True))
           - jnp.exp(jnp.sum(lq2_ref[...] * lk2_ref[...], axis=-1, keepdims=True)) + LAMBDA_INIT)
    nt = (((1,), (1,)), ((), ()))
    n_kt = SEQ // DIFF_TK
    n_chunks = SEQ // DIFF_TQ
    sublanes = d_scr.shape[2]

    def rows_of(c):
        start = c * DIFF_TQ
        return pl.ds(start if isinstance(c, int) else pl.multiple_of(start, DIFF_TQ), DIFF_TQ)

    def masked_halves(q):
        zero = jnp.zeros_like(q)
        return jnp.where(lo, q, zero), jnp.where(lo, zero, q)

    def sublane_partial_sum(e, acc):
        for r in range(e.shape[0] // sublanes):
            part = e[r * sublanes:(r + 1) * sublanes, :]
            acc = part if acc is None else acc + part
        return acc

    def weights_unshifted(c, slot):
        for half, q_half in enumerate(masked_halves(q_ref[rows_of(c), :])):
            s = lax.dot_general(k_ref[...], q_half, nt, preferred_element_type=F32)
            den = None
            for kt in range(n_kt):
                ks = slice(kt * DIFF_TK, (kt + 1) * DIFF_TK)
                e = jnp.exp2(s[ks, :])
                e_scr[slot, half, ks, :] = e.astype(BF16)
                den = sublane_partial_sum(e, den)
            d_scr[slot, half] = den

    def weights_max_shifted(c, slot):
        for half, q_half in enumerate(masked_halves(q_ref[rows_of(c), :])):
            s = lax.dot_general(k_ref[...], q_half, nt, preferred_element_type=F32)
            e = jnp.exp2(s - jnp.max(s, axis=0, keepdims=True))
            e_scr[slot, half] = e.astype(BF16)
            d_scr[slot, half] = sublane_partial_sum(e, None)

    def normalise_pv(c, slot):
        den_a = jnp.sum(d_scr[slot, 0], axis=0, keepdims=True)
        den_b = jnp.sum(d_scr[slot, 1], axis=0, keepdims=True)
        inv_a = (1.0 / den_a).astype(BF16)
        inv_b = (lam / den_b).astype(BF16)
        ot = None
        for kt in range(n_kt):
            ks = slice(kt * DIFF_TK, (kt + 1) * DIFF_TK)
            a = e_scr[slot, 0, ks, :] * inv_a - e_scr[slot, 1, ks, :] * inv_b
            pv = jnp.dot(vt_scr[:, ks], a, preferred_element_type=F32)
            ot = pv if ot is None else ot + pv
        o = ot.T
        y = o * lax.rsqrt(jnp.mean(o * o, axis=-1, keepdims=True) + EPS) * subln_ref[...]
        y = y * (1.0 - LAMBDA_INIT)
        z = z_ref[rows_of(c), :].astype(F32)
        o_ref[rows_of(c), :] = (y * (z * _sigmoid(z))).astype(BF16)
        return (jnp.min(jnp.minimum(den_a, den_b), axis=1, keepdims=True),
                jnp.max(jnp.maximum(den_a, den_b), axis=1, keepdims=True))

    weights_unshifted(0, 0)

    carry = (jnp.full((1, 1), SAFE_DEN_MAX, F32), jnp.full((1, 1), SAFE_DEN_MIN, F32))
    for c0 in range(0, n_chunks, 2):
        weights_unshifted(c0 + 1, 1)
        lo0, hi0 = normalise_pv(c0, 0)
        if c0 + 2 < n_chunks:
            weights_unshifted(c0 + 2, 0)
        lo1, hi1 = normalise_pv(c0 + 1, 1)
        carry = (jnp.minimum(carry[0], jnp.minimum(lo0, lo1)), jnp.maximum(carry[1], jnp.maximum(hi0, hi1)))
    d_min, d_max = carry
    in_range = (jnp.min(d_min) >= SAFE_DEN_MIN) & (jnp.max(d_max) <= SAFE_DEN_MAX)

    @pl.when(jnp.logical_not(in_range))
    def _():
        def body(c, carry):
            weights_max_shifted(c, 0)
            normalise_pv(c, 0)
            return carry

        lax.fori_loop(0, n_chunks, body, 0)


def _diff_attention(proj, lq1, lk1, lq2, lk2, subln):
    def col(base):
        return pl.BlockSpec((SEQ, LANES), lambda b, h: (b, base + h))

    small = pl.BlockSpec((1, HEAD_DIM), lambda b, h: (0, 0))
    return pl.pallas_call(
        _diff_kernel,
        grid=(BATCH, DIFF_HEADS),
        in_specs=[col(QD_B), col(KD_B), col(VD_B), col(ZD_B), small, small, small, small,
                  pl.BlockSpec((1, LANES), lambda b, h: (0, 0))],
        out_specs=pl.BlockSpec((SEQ, LANES), lambda b, h: (b, h)),
        out_shape=jax.ShapeDtypeStruct((BATCH * SEQ, DIFF_HEADS * LANES), BF16),
        scratch_shapes=[pltpu.VMEM((LANES, SEQ), BF16),
                        pltpu.VMEM((2, 2, SEQ, DIFF_TQ), BF16),
                        pltpu.VMEM((2, 2, SUBLANES, DIFF_TQ), F32)],
        compiler_params=pltpu.CompilerParams(dimension_semantics=("arbitrary", "arbitrary"),
                                             vmem_limit_bytes=VMEM_LIMIT),
        name="diff_attn",
    )(proj, proj, proj, proj, lq1, lk1, lq2, lk2, subln)


def _out_kernel(ya_ref, yd_ref, ga_ref, gb_ref, x_ref, gate_ref, wpa_ref, wpb_ref, wout_ref, npost_ref,
                o_ref):
    pa = jnp.dot(ya_ref[...], wpa_ref[...], preferred_element_type=F32)
    pb = jnp.dot(yd_ref[...], wpb_ref[...], preferred_element_type=F32)
    mix = _sigmoid(ga_ref[...].astype(F32)) * pa + _sigmoid(gb_ref[...].astype(F32)) * pb
    y = jnp.dot(mix.astype(BF16), wout_ref[...], preferred_element_type=F32)
    y = y * lax.rsqrt(jnp.mean(y * y, axis=-1, keepdims=True) + EPS) * npost_ref[...]
    o_ref[...] = x_ref[...] + gate_ref[...] * y


def _out_stage(ya, yd, proj, x2d, mod4, wpa, wpb, wout, norm_post):
    per_batch = SEQ // OUT_TM
    gm_block = GM_COL // D_MODEL
    const = lambda i: (0, 0)
    return pl.pallas_call(
        _out_kernel,
        grid=(BATCH * per_batch,),
        in_specs=[pl.BlockSpec((OUT_TM, DSW_HEADS_PER_GROUP * HEAD_DIM), lambda i: (i, 0)),
                  pl.BlockSpec((OUT_TM, D_MODEL), lambda i: (i, 0)),
                  pl.BlockSpec((OUT_TM, D_MODEL), lambda i: (i, gm_block)),
                  pl.BlockSpec((OUT_TM, D_MODEL), lambda i: (i, gm_block + 1)),
                  pl.BlockSpec((OUT_TM, D_MODEL), lambda i: (i, 0)),
                  pl.BlockSpec((None, None, 1, D_MODEL), lambda i: (i // per_batch, 2, 0, 0)),
                  pl.BlockSpec((DSW_HEADS_PER_GROUP * HEAD_DIM, D_MODEL), const),
                  pl.BlockSpec((D_MODEL, D_MODEL), const),
                  pl.BlockSpec((D_MODEL, D_MODEL), const),
                  pl.BlockSpec((1, D_MODEL), const)],
        out_specs=pl.BlockSpec((OUT_TM, D_MODEL), lambda i: (i, 0)),
        out_shape=jax.ShapeDtypeStruct((BATCH * SEQ, D_MODEL), F32),
        compiler_params=pltpu.CompilerParams(dimension_semantics=("arbitrary",),
                                             vmem_limit_bytes=VMEM_LIMIT),
        name="out_stage",
    )(ya, yd, proj, proj, x2d, mod4, wpa, wpb, wout, norm_post)


def kernel(x, c, positions, w_ada, b_ada, norm_pre, w_in, lambda_q1, lambda_k1, lambda_q2, lambda_k2,
           diff_subln, w_proj_a, w_proj_b, w_out, norm_post):
    assert x.shape == (BATCH, SEQ, D_MODEL) and w_in.shape == (1, D_MODEL, IN_WIDTH)
    x2d = x.reshape(BATCH * SEQ, D_MODEL)
    pos2d = jnp.repeat(positions.reshape(BATCH * SEQ // ROPE_PACK, ROPE_PACK), HEAD_DIM // 2, axis=1)
    half = HEAD_DIM // 2
    inv_freq = ROPE_THETA ** (-jnp.arange(half, dtype=F32) / half)
    inv_freq = jnp.tile(inv_freq, LANES // half).reshape(1, LANES)

    mod = _adaln(c, w_ada[0], b_ada)
    mod4 = mod.reshape(BATCH, 3, 1, D_MODEL)
    col = jnp.arange(IN_WIDTH)
    is_q_col = (col < KA_COL) | ((col >= QD_COL) & (col < KD_COL))
    col_scale = jnp.where(is_q_col, Q_SCALE, 1.0).astype(F32).reshape(1, IN_WIDTH)
    proj = _inproj(x2d, pos2d, mod4, norm_pre, inv_freq, _prep_w_in(w_in[0]), col_scale)
    ya = _dsw_attention(proj)
    yd = _diff_attention(proj, lambda_q1, lambda_k1, lambda_q2, lambda_k2, diff_subln)
    out = _out_stage(ya, yd, proj, x2d, mod4, w_proj_a[0].astype(BF16), w_proj_b[0].astype(BF16),
                     w_out[0].astype(BF16), norm_post)
    return out.reshape(BATCH, SEQ, D_MODEL)
```

```python
import math

import jax
import jax.numpy as jnp
from jax import lax
from jax.experimental import pallas as pl
from jax.experimental.pallas import tpu as pltpu

F32 = jnp.float32
BF16 = jnp.bfloat16

D_MODEL = 1024
BATCH = 16
SEQ = 2048
HEAD_DIM = 64
ROPE_THETA = 10000.0
EPS = 1e-6
NEG_INF = -1e30
DSW_DILATIONS = (1, 4, 16)
DSW_RADIUS = 64
DSW_HEADS_PER_GROUP = 8
DIFF_HEADS = 8
LAMBDA_INIT = 0.8 - 0.6 * math.exp(-0.3 * 0)

LANES = 128
SUBLANES = 8
IN_TILE_N = 1024
IN_CHUNK_M = 128
ROPE_PACK = LANES // (HEAD_DIM // 2)
Q_SCALE = HEAD_DIM ** -0.5 * math.log2(math.e)
QA_COL, KA_COL, VA_COL, ZA_COL, QD_COL, KD_COL, VD_COL, ZD_COL, GM_COL, IN_WIDTH = (
    0, 1536, 3072, 4608, 5120, 6144, 7168, 8192, 9216, 11264)
IN_FIRST_PLAIN_TILE = VA_COL // IN_TILE_N
assert VA_COL % IN_TILE_N == 0 and QD_COL % IN_TILE_N == 0 and VD_COL % IN_TILE_N == 0
QA_B, KA_B, VA_B, ZA_B, QD_B, KD_B, VD_B, ZD_B = (
    c // LANES for c in (QA_COL, KA_COL, VA_COL, ZA_COL, QD_COL, KD_COL, VD_COL, ZD_COL))
DSW_CHUNK = 128
FAST_STRIDE = 4
DIFF_TQ = 512
DIFF_TK = 256
SAFE_DEN_MIN = 2.0 ** -90
SAFE_DEN_MAX = 2.0 ** 115
OUT_TM = 1024
VMEM_LIMIT = 56 * 1024 * 1024


def _sigmoid(z):
    return 1.0 / (1.0 + jnp.exp(-z))


def _adaln_kernel(c_ref, w_ref, b_ref, o_ref):
    c = c_ref[...]
    sc = c * _sigmoid(c)
    o_ref[...] = jnp.dot(sc, w_ref[...], preferred_element_type=F32,
                         precision=lax.Precision.HIGHEST) + b_ref[...]


def _adaln(c, w_ada, b_ada):
    n_tiles = 3
    return pl.pallas_call(
        _adaln_kernel,
        grid=(n_tiles,),
        in_specs=[pl.BlockSpec((BATCH, D_MODEL), lambda n: (0, 0)),
                  pl.BlockSpec((D_MODEL, D_MODEL), lambda n: (0, n)),
                  pl.BlockSpec((1, D_MODEL), lambda n: (0, n))],
        out_specs=pl.BlockSpec((BATCH, D_MODEL), lambda n: (0, n)),
        out_shape=jax.ShapeDtypeStruct((BATCH, 3 * D_MODEL), F32),
        compiler_params=pltpu.CompilerParams(dimension_semantics=("arbitrary",),
                                             vmem_limit_bytes=VMEM_LIMIT),
        name="adaln",
    )(c, w_ada, b_ada)


def _is_rope_tile(tile):
    return (tile < IN_FIRST_PLAIN_TILE) | ((tile >= QD_COL // IN_TILE_N) & (tile < VD_COL // IN_TILE_N))


def _wprep_kernel(w_ref, o_ref):
    @pl.when(_is_rope_tile(pl.program_id(0)))
    def _():
        group = lax.broadcasted_iota(jnp.int32, (1, LANES), 1) // (HEAD_DIM // 2)
        for j in range(IN_TILE_N // LANES):
            a = w_ref[:, j * LANES:(j + 1) * LANES]
            swapped = jnp.where(group == 1, pltpu.roll(a, LANES - HEAD_DIM // 2, 1),
                                jnp.where(group == 2, pltpu.roll(a, HEAD_DIM // 2, 1), a))
            o_ref[:, j * LANES:(j + 1) * LANES] = swapped.astype(BF16)

    @pl.when(jnp.logical_not(_is_rope_tile(pl.program_id(0))))
    def _():
        o_ref[...] = w_ref[...].astype(BF16)


def _prep_w_in(w_in):
    return pl.pallas_call(
        _wprep_kernel,
        grid=(IN_WIDTH // IN_TILE_N,),
        in_specs=[pl.BlockSpec((D_MODEL, IN_TILE_N), lambda t: (0, t))],
        out_specs=pl.BlockSpec((D_MODEL, IN_TILE_N), lambda t: (0, t)),
        out_shape=jax.ShapeDtypeStruct((D_MODEL, IN_WIDTH), BF16),
        compiler_params=pltpu.CompilerParams(dimension_semantics=("arbitrary",),
                                             vmem_limit_bytes=VMEM_LIMIT),
        name="wprep",
    )(w_in)


def _inproj_tile(n):
    return jnp.where(n < 2, n + IN_FIRST_PLAIN_TILE, jnp.where(n < 2 + IN_FIRST_PLAIN_TILE, n - 2, n))


def _inproj_kernel(x_ref, pos_ref, shift_ref, scale_ref, npre_ref, invf_ref, w_ref, cs_ref, o_ref,
                   h_scr, cos_scr, sin_scr):
    n = pl.program_id(1)
    tile = _inproj_tile(n)
    is_rope = (n >= 2) & _is_rope_tile(tile)

    def chunks(before, epilogue):
        for c in range(SEQ // IN_CHUNK_M):
            rows = slice(c * IN_CHUNK_M, (c + 1) * IN_CHUNK_M)
            before(rows)
            acc = jnp.dot(h_scr[rows, :], w_ref[...], preferred_element_type=F32)
            epilogue(rows, acc)

    def nothing(rows):
        pass

    def plain(rows, acc):
        o_ref[rows, :] = acc.astype(BF16)

    def normalise_rows(rows):
        x = x_ref[rows, :]
        ms = jnp.mean(x * x, axis=-1, keepdims=True)
        y = x * lax.rsqrt(ms + EPS) * npre_ref[...]
        h_scr[rows, :] = (y * (1.0 + scale_ref[...]) + shift_ref[...]).astype(BF16)

    def rotary_tables(rows):
        n_packed = IN_CHUNK_M // ROPE_PACK
        ang = pos_ref[rows.start // ROPE_PACK:rows.start // ROPE_PACK + n_packed, :].astype(F32) * invf_ref[...]
        lane = lax.broadcasted_iota(jnp.int32, (1, LANES), 1)
        sgn = jnp.where(lane < LANES // 2, -1.0, 1.0)
        group = lane // (HEAD_DIM // 2)
        for table, dst, sign in ((jnp.cos(ang), cos_scr, None), (jnp.sin(ang), sin_scr, sgn)):
            for j in range(ROPE_PACK):
                own = jnp.where(group == j, table, 0.0)
                rep = own
                for shift in range(1, ROPE_PACK):
                    rep = rep + pltpu.roll(own, shift * (HEAD_DIM // 2), 1)
                if sign is not None:
                    rep = rep * sign
                dst[pl.ds(rows.start + j, n_packed, stride=ROPE_PACK), :] = rep

    def rope(rows, acc):
        cos, sin = cos_scr[rows, :], sin_scr[rows, :]
        for j in range(IN_TILE_N // LANES):
            a = acc[:, j * LANES:(j + 1) * LANES] * cs_ref[:, j * LANES:(j + 1) * LANES]
            o_ref[rows, j * LANES:(j + 1) * LANES] = (
                a * cos + pltpu.roll(a, LANES // 2, 1) * sin).astype(BF16)

    pl.when(n == 0)(lambda: chunks(normalise_rows, plain))
    pl.when(n == 1)(lambda: chunks(rotary_tables, plain))
    pl.when(is_rope)(lambda: chunks(nothing, rope))
    pl.when((n >= 2) & jnp.logical_not(is_rope))(lambda: chunks(nothing, plain))


def _inproj(x2d, pos2d, mod4, norm_pre, inv_freq, w_in_bf16, col_scale):
    return pl.pallas_call(
        _inproj_kernel,
        grid=(BATCH, IN_WIDTH // IN_TILE_N),
        in_specs=[pl.BlockSpec((SEQ, D_MODEL), lambda b, n: (b, 0)),
                  pl.BlockSpec((SEQ // ROPE_PACK, LANES), lambda b, n: (b, 0)),
                  pl.BlockSpec((None, None, 1, D_MODEL), lambda b, n: (b, 0, 0, 0)),
                  pl.BlockSpec((None, None, 1, D_MODEL), lambda b, n: (b, 1, 0, 0)),
                  pl.BlockSpec((1, D_MODEL), lambda b, n: (0, 0)),
                  pl.BlockSpec((1, LANES), lambda b, n: (0, 0)),
                  pl.BlockSpec((D_MODEL, IN_TILE_N), lambda b, n: (0, _inproj_tile(n))),
                  pl.BlockSpec((1, IN_TILE_N), lambda b, n: (0, _inproj_tile(n)))],
        out_specs=pl.BlockSpec((SEQ, IN_TILE_N), lambda b, n: (b, _inproj_tile(n))),
        out_shape=jax.ShapeDtypeStruct((BATCH * SEQ, IN_WIDTH), BF16),
        scratch_shapes=[pltpu.VMEM((SEQ, D_MODEL), BF16),
                        pltpu.VMEM((SEQ, LANES), F32),
                        pltpu.VMEM((SEQ, LANES), F32)],
        compiler_params=pltpu.CompilerParams(dimension_semantics=("arbitrary", "arbitrary"),
                                             vmem_limit_bytes=VMEM_LIMIT),
        name="inproj",
    )(x2d, pos2d, mod4, mod4, norm_pre, inv_freq, w_in_bf16, col_scale)


def _dsw_kernel(q0_ref, k0_ref, v0_ref, q1_ref, k1_ref, v1_ref, q2_ref, k2_ref, v2_ref,
                za_ref, o_ref, qf_scr, kf_scr, vf_scr, q4_scr, k4_scr, v4_scr, acc_scr, y4_scr,
                num_scr, max_scr, den_scr):
    lane = lax.broadcasted_iota(jnp.int32, (1, LANES), 1)
    lo = lane < HEAD_DIM
    qk_lo = (lane % HEAD_DIM) < HEAD_DIM // 2
    groups = ((q0_ref, k0_ref, v0_ref), (q1_ref, k1_ref, v1_ref), (q2_ref, k2_ref, v2_ref))
    nt = (((1,), (1,)), ((), ()))
    comb_rows = 256

    def geometry(g):
        dil = DSW_DILATIONS[g]
        sub_len = SEQ // dil
        win = min(2 * DSW_CHUNK, sub_len)
        return dil, sub_len, win

    def band_mask(q_start, k_start, win):
        qi = lax.broadcasted_iota(jnp.int32, (DSW_CHUNK, win), 0) + q_start
        kj = lax.broadcasted_iota(jnp.int32, (DSW_CHUNK, win), 1) + k_start
        return jnp.abs(kj - qi) <= DSW_RADIUS

    for g, (q_ref, k_ref, v_ref) in enumerate(groups):
        dil, sub_len, win = geometry(g)
        two_hop = dil > FAST_STRIDE
        inner = dil // FAST_STRIDE
        quarter = SEQ // FAST_STRIDE
        q_src, k_src, v_src, out_scr = qf_scr, kf_scr, vf_scr, acc_scr
        if dil > 1:
            qf_scr[...] = q_ref[...].astype(F32)
            kf_scr[...] = k_ref[...].astype(F32)
            vf_scr[...] = v_ref[...].astype(F32)
        if two_hop:
            assert sub_len == DSW_CHUNK
            q_src, k_src, v_src, out_scr = q4_scr, k4_scr, v4_scr, y4_scr
            for r in range(FAST_STRIDE):
                block, every = slice(r * quarter, (r + 1) * quarter), pl.ds(r, quarter, stride=FAST_STRIDE)
                q4_scr[block, :] = qf_scr[every, :]
                k4_scr[block, :] = kf_scr[every, :]
                v4_scr[block, :] = vf_scr[every, :]
        for res in range(dil):
            for c in range(sub_len // DSW_CHUNK):
                q_start = c * DSW_CHUNK
                k_start = min(max(q_start - DSW_RADIUS, 0), sub_len - win)
                if dil == 1:
                    q = q_ref[q_start:q_start + DSW_CHUNK, :]
                    kw = k_ref[k_start:k_start + win, :]
                    vw = v_ref[k_start:k_start + win, :]
                    rows = pl.ds(q_start, DSW_CHUNK)
                else:
                    if two_hop:
                        rows = krows = pl.ds((res % FAST_STRIDE) * quarter + res // FAST_STRIDE, DSW_CHUNK,
                                             stride=inner)
                    else:
                        rows = pl.ds(res + dil * q_start, DSW_CHUNK, stride=dil)
                        krows = pl.ds(res + dil * k_start, win, stride=dil)
                    q = q_src[rows, :].astype(BF16)
                    kw = k_src[krows, :].astype(BF16)
                    vw = v_src[krows, :].astype(BF16)
                zq = jnp.zeros_like(q)
                q2 = jnp.concatenate([jnp.where(qk_lo, q, zq), jnp.where(qk_lo, zq, q)], axis=0)
                s = lax.dot_general(q2, kw, nt, preferred_element_type=F32)
                band = band_mask(q_start, k_start, win)
                e = jnp.where(jnp.concatenate([band, band], axis=0), jnp.exp2(s), 0.0).astype(BF16)
                one = jnp.ones_like(vw)
                r_a = jnp.dot(e[:DSW_CHUNK], jnp.where(lo, vw, one), preferred_element_type=F32)
                r_b = jnp.dot(e[DSW_CHUNK:], jnp.where(lo, one, vw), preferred_element_type=F32)
                if g == 0 or two_hop:
                    out_scr[0, rows, :] = r_a
                    out_scr[1, rows, :] = r_b
                else:
                    acc_scr[0, rows, :] = acc_scr[0, rows, :] + r_a
                    acc_scr[1, rows, :] = acc_scr[1, rows, :] + r_b
        if two_hop:
            for r in range(FAST_STRIDE):
                block, every = slice(r * quarter, (r + 1) * quarter), pl.ds(r, quarter, stride=FAST_STRIDE)
                acc_scr[0, every, :] = acc_scr[0, every, :] + y4_scr[0, block, :]
                acc_scr[1, every, :] = acc_scr[1, every, :] + y4_scr[1, block, :]

    def combine_unshifted(i, carry):
        d_min, d_max = carry
        r = pl.ds(pl.multiple_of(i * comb_rows, comb_rows), comb_rows)
        acc_a, acc_b = acc_scr[0, r, :], acc_scr[1, r, :]
        num = jnp.where(lo, acc_a, acc_b)
        den = pltpu.roll(jnp.where(lo, acc_b, acc_a), HEAD_DIM, 1)
        z = za_ref[r, :].astype(F32)
        o_ref[r, :] = (num / den * (z * _sigmoid(z))).astype(BF16)
        return (jnp.minimum(d_min, jnp.min(den, axis=0, keepdims=True)),
                jnp.maximum(d_max, jnp.max(den, axis=0, keepdims=True)))

    d_min, d_max = lax.fori_loop(
        0, SEQ // comb_rows, combine_unshifted,
        (jnp.full((1, LANES), SAFE_DEN_MAX, F32), jnp.full((1, LANES), SAFE_DEN_MIN, F32)))
    in_range = (jnp.min(d_min) >= SAFE_DEN_MIN) & (jnp.max(d_max) <= SAFE_DEN_MAX)

    @pl.when(jnp.logical_not(in_range))
    def _():
        for g, (q_ref, k_ref, v_ref) in enumerate(groups):
            dil, sub_len, win = geometry(g)
            n_c = sub_len // DSW_CHUNK
            qf_scr[...] = q_ref[...].astype(F32)
            kf_scr[...] = k_ref[...].astype(F32)
            vf_scr[...] = v_ref[...].astype(F32)

            def chunk(t, carry, g=g, dil=dil, sub_len=sub_len, win=win, n_c=n_c):
                res = t // n_c
                q_start = (t % n_c) * DSW_CHUNK
                k_start = jnp.clip(q_start - DSW_RADIUS, 0, sub_len - win)
                rows = pl.ds(res + dil * q_start, DSW_CHUNK, stride=dil)
                krows = pl.ds(res + dil * k_start, win, stride=dil)
                q = qf_scr[rows, :].astype(BF16)
                kw = kf_scr[krows, :].astype(BF16)
                vw = vf_scr[krows, :].astype(BF16)
                zero = jnp.zeros_like(kw)
                band = band_mask(q_start, k_start, win)
                s_a = lax.dot_general(q, jnp.where(qk_lo, kw, zero), nt, preferred_element_type=F32)
                s_b = lax.dot_general(q, jnp.where(qk_lo, zero, kw), nt, preferred_element_type=F32)
                s_a = jnp.where(band, s_a, NEG_INF)
                s_b = jnp.where(band, s_b, NEG_INF)
                m_a = jnp.max(s_a, axis=-1, keepdims=True)
                m_b = jnp.max(s_b, axis=-1, keepdims=True)
                e_a = jnp.exp2(s_a - m_a)
                e_b = jnp.exp2(s_b - m_b)
                d_a = jnp.sum(e_a, axis=-1, keepdims=True)
                d_b = jnp.sum(e_b, axis=-1, keepdims=True)
                num = (jnp.dot(e_a.astype(BF16), jnp.where(lo, vw, zero), preferred_element_type=F32)
                       + jnp.dot(e_b.astype(BF16), jnp.where(lo, zero, vw), preferred_element_type=F32))
                num_scr[g, rows, :] = num
                max_scr[g, rows, :] = jnp.where(lo, m_a, m_b)
                den_scr[g, rows, :] = jnp.where(lo, d_a, d_b)
                return carry

            lax.fori_loop(0, dil * n_c, chunk, 0)

        def combine(i, carry):
            r = pl.ds(pl.multiple_of(i * comb_rows, comb_rows), comb_rows)
            m0, m1, m2 = max_scr[0, r, :], max_scr[1, r, :], max_scr[2, r, :]
            m = jnp.maximum(jnp.maximum(m0, m1), m2)
            w0, w1, w2 = jnp.exp2(m0 - m), jnp.exp2(m1 - m), jnp.exp2(m2 - m)
            num = w0 * num_scr[0, r, :] + w1 * num_scr[1, r, :] + w2 * num_scr[2, r, :]
            den = w0 * den_scr[0, r, :] + w1 * den_scr[1, r, :] + w2 * den_scr[2, r, :]
            z = za_ref[r, :].astype(F32)
            o_ref[r, :] = (num / den * (z * _sigmoid(z))).astype(BF16)
            return carry

        lax.fori_loop(0, SEQ // comb_rows, combine, 0)


def _dsw_attention(proj):
    def col(base, g):
        return pl.BlockSpec((SEQ, LANES), lambda b, hp: (b, base + 4 * g + hp))

    in_specs = []
    for g in range(3):
        in_specs += [col(QA_B, g), col(KA_B, g), col(VA_B, g)]
    in_specs.append(pl.BlockSpec((SEQ, LANES), lambda b, hp: (b, ZA_B + hp)))
    return pl.pallas_call(
        _dsw_kernel,
        grid=(BATCH, DSW_HEADS_PER_GROUP // 2),
        in_specs=in_specs,
        out_specs=pl.BlockSpec((SEQ, LANES), lambda b, hp: (b, hp)),
        out_shape=jax.ShapeDtypeStruct((BATCH * SEQ, DSW_HEADS_PER_GROUP * HEAD_DIM), BF16),
        scratch_shapes=[pltpu.VMEM((SEQ, LANES), F32)] * 6
        + [pltpu.VMEM((2, SEQ, LANES), F32)] * 2
        + [pltpu.VMEM((3, SEQ, LANES), F32)] * 3,
        compiler_params=pltpu.CompilerParams(dimension_semantics=("arbitrary", "arbitrary"),
                                             vmem_limit_bytes=VMEM_LIMIT),
        name="dsw_attn",
    )(*([proj] * 10))


def _diff_kernel(q_ref, k_ref, v_ref, z_ref, lq1_ref, lk1_ref, lq2_ref, lk2_ref, subln_ref, o_ref,
                 vt_scr, e_scr, d_scr):
    lane = lax.broadcasted_iota(jnp.int32, (1, LANES), 1)
    lo = (lane % HEAD_DIM) < HEAD_DIM // 2
    vt_scr[...] = v_ref[...].T
    lam = (jnp.exp(jnp.sum(lq1_ref[...] * lk1_ref[...], axis=-1, keepdims=True))
           - jnp.exp(jnp.sum(lq2_ref[...] * lk2_ref[...], axis=-1, keepdims=True)) + LAMBDA_INIT)
    nt = (((1,), (1,)), ((), ()))
    n_kt = SEQ // DIFF_TK
    n_chunks = SEQ // DIFF_TQ
    sublanes = d_scr.shape[2]

    def rows_of(c):
        start = c * DIFF_TQ
        return pl.ds(start if isinstance(c, int) else pl.multiple_of(start, DIFF_TQ), DIFF_TQ)

    def masked_halves(q):
        zero = jnp.zeros_like(q)
        return jnp.where(lo, q, zero), jnp.where(lo, zero, q)

    def sublane_partial_sum(e, acc):
        for r in range(e.shape[0] // sublanes):
            part = e[r * sublanes:(r + 1) * sublanes, :]
            acc = part if acc is None else acc + part
        return acc

    def weights_unshifted(c, slot):
        for half, q_half in enumerate(masked_halves(q_ref[rows_of(c), :])):
            s = lax.dot_general(k_ref[...], q_half, nt, preferred_element_type=F32)
            den = None
            for kt in range(n_kt):
                ks = slice(kt * DIFF_TK, (kt + 1) * DIFF_TK)
                e = jnp.exp2(s[ks, :])
                e_scr[slot, half, ks, :] = e.astype(BF16)
                den = sublane_partial_sum(e, den)
            d_scr[slot, half] = den

    def weights_max_shifted(c, slot):
        for half, q_half in enumerate(masked_halves(q_ref[rows_of(c), :])):
            s = lax.dot_general(k_ref[...], q_half, nt, preferred_element_type=F32)
            e = jnp.exp2(s - jnp.max(s, axis=0, keepdims=True))
            e_scr[slot, half] = e.astype(BF16)
            d_scr[slot, half] = sublane_partial_sum(e, None)

    def normalise_pv(c, slot):
        den_a = jnp.sum(d_scr[slot, 0], axis=0, keepdims=True)
        den_b = jnp.sum(d_scr[slot, 1], axis=0, keepdims=True)
        inv_a = (1.0 / den_a).astype(BF16)
        inv_b = (lam / den_b).astype(BF16)
        ot = None
        for kt in range(n_kt):
            ks = slice(kt * DIFF_TK, (kt + 1) * DIFF_TK)
            a = e_scr[slot, 0, ks, :] * inv_a - e_scr[slot, 1, ks, :] * inv_b
            pv = jnp.dot(vt_scr[:, ks], a, preferred_element_type=F32)
            ot = pv if ot is None else ot + pv
        o = ot.T
        y = o * lax.rsqrt(jnp.mean(o * o, axis=-1, keepdims=True) + EPS) * subln_ref[...]
        y = y * (1.0 - LAMBDA_INIT)
        z = z_ref[rows_of(c), :].astype(F32)
        o_ref[rows_of(c), :] = (y * (z * _sigmoid(z))).astype(BF16)
        return (jnp.min(jnp.minimum(den_a, den_b), axis=1, keepdims=True),
                jnp.max(jnp.maximum(den_a, den_b), axis=1, keepdims=True))

    weights_unshifted(0, 0)

    carry = (jnp.full((1, 1), SAFE_DEN_MAX, F32), jnp.full((1, 1), SAFE_DEN_MIN, F32))
    for c0 in range(0, n_chunks, 2):
        weights_unshifted(c0 + 1, 1)
        lo0, hi0 = normalise_pv(c0, 0)
        if c0 + 2 < n_chunks:
            weights_unshifted(c0 + 2, 0)
        lo1, hi1 = normalise_pv(c0 + 1, 1)
        carry = (jnp.minimum(carry[0], jnp.minimum(lo0, lo1)), jnp.maximum(carry[1], jnp.maximum(hi0, hi1)))
    d_min, d_max = carry
    in_range = (jnp.min(d_min) >= SAFE_DEN_MIN) & (jnp.max(d_max) <= SAFE_DEN_MAX)

    @pl.when(jnp.logical_not(in_range))
    def _():
        def body(c, carry):
            weights_max_shifted(c, 0)
            normalise_pv(c, 0)
            return carry

        lax.fori_loop(0, n_chunks, body, 0)


def _diff_attention(proj, lq1, lk1, lq2, lk2, subln):
    def col(base):
        return pl.BlockSpec((SEQ, LANES), lambda b, h: (b, base + h))

    small = pl.BlockSpec((1, HEAD_DIM), lambda b, h: (0, 0))
    return pl.pallas_call(
        _diff_kernel,
        grid=(BATCH, DIFF_HEADS),
        in_specs=[col(QD_B), col(KD_B), col(VD_B), col(ZD_B), small, small, small, small,
                  pl.BlockSpec((1, LANES), lambda b, h: (0, 0))],
        out_specs=pl.BlockSpec((SEQ, LANES), lambda b, h: (b, h)),
        out_shape=jax.ShapeDtypeStruct((BATCH * SEQ, DIFF_HEADS * LANES), BF16),
        scratch_shapes=[pltpu.VMEM((LANES, SEQ), BF16),
                        pltpu.VMEM((2, 2, SEQ, DIFF_TQ), BF16),
                        pltpu.VMEM((2, 2, SUBLANES, DIFF_TQ), F32)],
        compiler_params=pltpu.CompilerParams(dimension_semantics=("arbitrary", "arbitrary"),
                                             vmem_limit_bytes=VMEM_LIMIT),
        name="diff_attn",
    )(proj, proj, proj, proj, lq1, lk1, lq2, lk2, subln)


def _out_kernel(ya_ref, yd_ref, ga_ref, gb_ref, x_ref, gate_ref, wpa_ref, wpb_ref, wout_ref, npost_ref,
                o_ref):
    pa = jnp.dot(ya_ref[...], wpa_ref[...], preferred_element_type=F32)
    pb = jnp.dot(yd_ref[...], wpb_ref[...], preferred_element_type=F32)
    mix = _sigmoid(ga_ref[...].astype(F32)) * pa + _sigmoid(gb_ref[...].astype(F32)) * pb
    y = jnp.dot(mix.astype(BF16), wout_ref[...], preferred_element_type=F32)
    y = y * lax.rsqrt(jnp.mean(y * y, axis=-1, keepdims=True) + EPS) * npost_ref[...]
    o_ref[...] = x_ref[...] + gate_ref[...] * y


def _out_stage(ya, yd, proj, x2d, mod4, wpa, wpb, wout, norm_post):
    per_batch = SEQ // OUT_TM
    gm_block = GM_COL // D_MODEL
    const = lambda i: (0, 0)
    return pl.pallas_call(
        _out_kernel,
        grid=(BATCH * per_batch,),
        in_specs=[pl.BlockSpec((OUT_TM, DSW_HEADS_PER_GROUP * HEAD_DIM), lambda i: (i, 0)),
                  pl.BlockSpec((OUT_TM, D_MODEL), lambda i: (i, 0)),
                  pl.BlockSpec((OUT_TM, D_MODEL), lambda i: (i, gm_block)),
                  pl.BlockSpec((OUT_TM, D_MODEL), lambda i: (i, gm_block + 1)),
                  pl.BlockSpec((OUT_TM, D_MODEL), lambda i: (i, 0)),
                  pl.BlockSpec((None, None, 1, D_MODEL), lambda i: (i // per_batch, 2, 0, 0)),
                  pl.BlockSpec((DSW_HEADS_PER_GROUP * HEAD_DIM, D_MODEL), const),
                  pl.BlockSpec((D_MODEL, D_MODEL), const),
                  pl.BlockSpec((D_MODEL, D_MODEL), const),
                  pl.BlockSpec((1, D_MODEL), const)],
        out_specs=pl.BlockSpec((OUT_TM, D_MODEL), lambda i: (i, 0)),
        out_shape=jax.ShapeDtypeStruct((BATCH * SEQ, D_MODEL), F32),
        compiler_params=pltpu.CompilerParams(dimension_semantics=("arbitrary",),
                                             vmem_limit_bytes=VMEM_LIMIT),
        name="out_stage",
    )(ya, yd, proj, proj, x2d, mod4, wpa, wpb, wout, norm_post)


def kernel(x, c, positions, w_ada, b_ada, norm_pre, w_in, lambda_q1, lambda_k1, lambda_q2, lambda_k2,
           diff_subln, w_proj_a, w_proj_b, w_out, norm_post):
    assert x.shape == (BATCH, SEQ, D_MODEL) and w_in.shape == (1, D_MODEL, IN_WIDTH)
    x2d = x.reshape(BATCH * SEQ, D_MODEL)
    pos2d = jnp.repeat(positions.reshape(BATCH * SEQ // ROPE_PACK, ROPE_PACK), HEAD_DIM // 2, axis=1)
    half = HEAD_DIM // 2
    inv_freq = ROPE_THETA ** (-jnp.arange(half, dtype=F32) / half)
    inv_freq = jnp.tile(inv_freq, LANES // half).reshape(1, LANES)

    mod = _adaln(c, w_ada[0], b_ada)
    mod4 = mod.reshape(BATCH, 3, 1, D_MODEL)
    col = jnp.arange(IN_WIDTH)
    is_q_col = (col < KA_COL) | ((col >= QD_COL) & (col < KD_COL))
    col_scale = jnp.where(is_q_col, Q_SCALE, 1.0).astype(F32).reshape(1, IN_WIDTH)
    proj = _inproj(x2d, pos2d, mod4, norm_pre, inv_freq, _prep_w_in(w_in[0]), col_scale)
    ya = _dsw_attention(proj)
    yd = _diff_attention(proj, lambda_q1, lambda_k1, lambda_q2, lambda_k2, diff_subln)
    out = _out_stage(ya, yd, proj, x2d, mod4, w_proj_a[0].astype(BF16), w_proj_b[0].astype(BF16),
                     w_out[0].astype(BF16), norm_post)
    return out.reshape(BATCH, SEQ, D_MODEL)
```

```python
import math

import jax
import jax.numpy as jnp
from jax import lax
from jax.experimental import pallas as pl
from jax.experimental.pallas import tpu as pltpu

F32 = jnp.float32
BF16 = jnp.bfloat16

D_MODEL = 1024
BATCH = 16
SEQ = 2048
HEAD_DIM = 64
ROPE_THETA = 10000.0
EPS = 1e-6
NEG_INF = -1e30
DSW_DILATIONS = (1, 4, 16)
DSW_RADIUS = 64
DSW_HEADS_PER_GROUP = 8
DIFF_HEADS = 8
LAMBDA_INIT = 0.8 - 0.6 * math.exp(-0.3 * 0)

LANES = 128
SUBLANES = 8
IN_TILE_N = 1024
IN_CHUNK_M = 128
ROPE_PACK = LANES // (HEAD_DIM // 2)
Q_SCALE = HEAD_DIM ** -0.5 * math.log2(math.e)
QA_COL, KA_COL, VA_COL, ZA_COL, QD_COL, KD_COL, VD_COL, ZD_COL, GM_COL, IN_WIDTH = (
    0, 1536, 3072, 4608, 5120, 6144, 7168, 8192, 9216, 11264)
IN_FIRST_PLAIN_TILE = VA_COL // IN_TILE_N
assert VA_COL % IN_TILE_N == 0 and QD_COL % IN_TILE_N == 0 and VD_COL % IN_TILE_N == 0
QA_B, KA_B, VA_B, ZA_B, QD_B, KD_B, VD_B, ZD_B = (
    c // LANES for c in (QA_COL, KA_COL, VA_COL, ZA_COL, QD_COL, KD_COL, VD_COL, ZD_COL))
DSW_CHUNK = 128
FAST_STRIDE = 4
DIFF_TQ = 512
DIFF_TK = 256
SAFE_DEN_MIN = 2.0 ** -90
SAFE_DEN_MAX = 2.0 ** 115
OUT_TM = 1024
VMEM_LIMIT = 56 * 1024 * 1024


def _sigmoid(z):
    return 1.0 / (1.0 + jnp.exp(-z))


def _adaln_kernel(c_ref, w_ref, b_ref, o_ref):
    c = c_ref[...]
    sc = c * _sigmoid(c)
    o_ref[...] = jnp.dot(sc, w_ref[...], preferred_element_type=F32,
                         precision=lax.Precision.HIGHEST) + b_ref[...]


def _adaln(c, w_ada, b_ada):
    n_tiles = 3
    return pl.pallas_call(
        _adaln_kernel,
        grid=(n_tiles,),
        in_specs=[pl.BlockSpec((BATCH, D_MODEL), lambda n: (0, 0)),
                  pl.BlockSpec((D_MODEL, D_MODEL), lambda n: (0, n)),
                  pl.BlockSpec((1, D_MODEL), lambda n: (0, n))],
        out_specs=pl.BlockSpec((BATCH, D_MODEL), lambda n: (0, n)),
        out_shape=jax.ShapeDtypeStruct((BATCH, 3 * D_MODEL), F32),
        compiler_params=pltpu.CompilerParams(dimension_semantics=("arbitrary",),
                                             vmem_limit_bytes=VMEM_LIMIT),
        name="adaln",
    )(c, w_ada, b_ada)


def _is_rope_tile(tile):
    return (tile < IN_FIRST_PLAIN_TILE) | ((tile >= QD_COL // IN_TILE_N) & (tile < VD_COL // IN_TILE_N))


def _wprep_kernel(w_ref, o_ref):
    @pl.when(_is_rope_tile(pl.program_id(0)))
    def _():
        group = lax.broadcasted_iota(jnp.int32, (1, LANES), 1) // (HEAD_DIM // 2)
        for j in range(IN_TILE_N // LANES):
            a = w_ref[:, j * LANES:(j + 1) * LANES]
            swapped = jnp.where(group == 1, pltpu.roll(a, LANES - HEAD_DIM // 2, 1),
                                jnp.where(group == 2, pltpu.roll(a, HEAD_DIM // 2, 1), a))
            o_ref[:, j * LANES:(j + 1) * LANES] = swapped.astype(BF16)

    @pl.when(jnp.logical_not(_is_rope_tile(pl.program_id(0))))
    def _():
        o_ref[...] = w_ref[...].astype(BF16)


def _prep_w_in(w_in):
    return pl.pallas_call(
        _wprep_kernel,
        grid=(IN_WIDTH // IN_TILE_N,),
        in_specs=[pl.BlockSpec((D_MODEL, IN_TILE_N), lambda t: (0, t))],
        out_specs=pl.BlockSpec((D_MODEL, IN_TILE_N), lambda t: (0, t)),
        out_shape=jax.ShapeDtypeStruct((D_MODEL, IN_WIDTH), BF16),
        compiler_params=pltpu.CompilerParams(dimension_semantics=("arbitrary",),
                                             vmem_limit_bytes=VMEM_LIMIT),
        name="wprep",
    )(w_in)


def _inproj_tile(n):
    return jnp.where(n < 2, n + IN_FIRST_PLAIN_TILE, jnp.where(n < 2 + IN_FIRST_PLAIN_TILE, n - 2, n))


def _inproj_kernel(x_ref, pos_ref, shift_ref, scale_ref, npre_ref, invf_ref, w_ref, cs_ref, o_ref,
                   h_scr, cos_scr, sin_scr):
    n = pl.program_id(1)
    tile = _inproj_tile(n)
    is_rope = (n >= 2) & _is_rope_tile(tile)

    def chunks(before, epilogue):
        for c in range(SEQ // IN_CHUNK_M):
            rows = slice(c * IN_CHUNK_M, (c + 1) * IN_CHUNK_M)
            before(rows)
            acc = jnp.dot(h_scr[rows, :], w_ref[...], preferred_element_type=F32)
            epilogue(rows, acc)

    def nothing(rows):
        pass

    def plain(rows, acc):
        o_ref[rows, :] = acc.astype(BF16)

    def normalise_rows(rows):
        x = x_ref[rows, :]
        ms = jnp.mean(x * x, axis=-1, keepdims=True)
        y = x * lax.rsqrt(ms + EPS) * npre_ref[...]
        h_scr[rows, :] = (y * (1.0 + scale_ref[...]) + shift_ref[...]).astype(BF16)

    def rotary_tables(rows):
        n_packed = IN_CHUNK_M // ROPE_PACK
        ang = pos_ref[rows.start // ROPE_PACK:rows.start // ROPE_PACK + n_packed, :].astype(F32) * invf_ref[...]
        lane = lax.broadcasted_iota(jnp.int32, (1, LANES), 1)
        sgn = jnp.where(lane < LANES // 2, -1.0, 1.0)
        group = lane // (HEAD_DIM // 2)
        for table, dst, sign in ((jnp.cos(ang), cos_scr, None), (jnp.sin(ang), sin_scr, sgn)):
            for j in range(ROPE_PACK):
                own = jnp.where(group == j, table, 0.0)
                rep = own
                for shift in range(1, ROPE_PACK):
                    rep = rep + pltpu.roll(own, shift * (HEAD_DIM // 2), 1)
                if sign is not None:
                    rep = rep * sign
                dst[pl.ds(rows.start + j, n_packed, stride=ROPE_PACK), :] = rep

    def rope(rows, acc):
        cos, sin = cos_scr[rows, :], sin_scr[rows, :]
        for j in range(IN_TILE_N // LANES):
            a = acc[:, j * LANES:(j + 1) * LANES] * cs_ref[:, j * LANES:(j + 1) * LANES]
            o_ref[rows, j * LANES:(j + 1) * LANES] = (
                a * cos + pltpu.roll(a, LANES // 2, 1) * sin).astype(BF16)

    pl.when(n == 0)(lambda: chunks(normalise_rows, plain))
    pl.when(n == 1)(lambda: chunks(rotary_tables, plain))
    pl.when(is_rope)(lambda: chunks(nothing, rope))
    pl.when((n >= 2) & jnp.logical_not(is_rope))(lambda: chunks(nothing, plain))


def _inproj(x2d, pos2d, mod4, norm_pre, inv_freq, w_in_bf16, col_scale):
    return pl.pallas_call(
        _inproj_kernel,
        grid=(BATCH, IN_WIDTH // IN_TILE_N),
        in_specs=[pl.BlockSpec((SEQ, D_MODEL), lambda b, n: (b, 0)),
                  pl.BlockSpec((SEQ // ROPE_PACK, LANES), lambda b, n: (b, 0)),
                  pl.BlockSpec((None, None, 1, D_MODEL), lambda b, n: (b, 0, 0, 0)),
                  pl.BlockSpec((None, None, 1, D_MODEL), lambda b, n: (b, 1, 0, 0)),
                  pl.BlockSpec((1, D_MODEL), lambda b, n: (0, 0)),
                  pl.BlockSpec((1, LANES), lambda b, n: (0, 0)),
                  pl.BlockSpec((D_MODEL, IN_TILE_N), lambda b, n: (0, _inproj_tile(n))),
                  pl.BlockSpec((1, IN_TILE_N), lambda b, n: (0, _inproj_tile(n)))],
        out_specs=pl.BlockSpec((SEQ, IN_TILE_N), lambda b, n: (b, _inproj_tile(n))),
        out_shape=jax.ShapeDtypeStruct((BATCH * SEQ, IN_WIDTH), BF16),
        scratch_shapes=[pltpu.VMEM((SEQ, D_MODEL), BF16),
                        pltpu.VMEM((SEQ, LANES), F32),
                        pltpu.VMEM((SEQ, LANES), F32)],
        compiler_params=pltpu.CompilerParams(dimension_semantics=("arbitrary", "arbitrary"),
                                             vmem_limit_bytes=VMEM_LIMIT),
        name="inproj",
    )(x2d, pos2d, mod4, mod4, norm_pre, inv_freq, w_in_bf16, col_scale)


def _dsw_kernel(q0_ref, k0_ref, v0_ref, q1_ref, k1_ref, v1_ref, q2_ref, k2_ref, v2_ref,
                za_ref, o_ref, qf_scr, kf_scr, vf_scr, q4_scr, k4_scr, v4_scr, acc_scr, y4_scr,
                num_scr, max_scr, den_scr):
    lane = lax.broadcasted_iota(jnp.int32, (1, LANES), 1)
    lo = lane < HEAD_DIM
    qk_lo = (lane % HEAD_DIM) < HEAD_DIM // 2
    groups = ((q0_ref, k0_ref, v0_ref), (q1_ref, k1_ref, v1_ref), (q2_ref, k2_ref, v2_ref))
    nt = (((1,), (1,)), ((), ()))
    comb_rows = 256

    def geometry(g):
        dil = DSW_DILATIONS[g]
        sub_len = SEQ // dil
        win = min(2 * DSW_CHUNK, sub_len)
        return dil, sub_len, win

    def band_mask(q_start, k_start, win):
        kj_minus_qi = (lax.broadcasted_iota(jnp.int32, (DSW_CHUNK, win), 1)
                       - lax.broadcasted_iota(jnp.int32, (DSW_CHUNK, win), 0))
        return jnp.abs(kj_minus_qi + (k_start - q_start)) <= DSW_RADIUS

    for g, (q_ref, k_ref, v_ref) in enumerate(groups):
        dil, sub_len, win = geometry(g)
        two_hop = dil > FAST_STRIDE
        inner = dil // FAST_STRIDE
        quarter = SEQ // FAST_STRIDE
        q_src, k_src, v_src, out_scr = qf_scr, kf_scr, vf_scr, acc_scr
        if dil > 1:
            qf_scr[...] = q_ref[...].astype(F32)
            kf_scr[...] = k_ref[...].astype(F32)
            vf_scr[...] = v_ref[...].astype(F32)
        if two_hop:
            assert sub_len == DSW_CHUNK
            q_src, k_src, v_src, out_scr = q4_scr, k4_scr, v4_scr, y4_scr
            for r in range(FAST_STRIDE):
                block, every = slice(r * quarter, (r + 1) * quarter), pl.ds(r, quarter, stride=FAST_STRIDE)
                q4_scr[block, :] = qf_scr[every, :]
                k4_scr[block, :] = kf_scr[every, :]
                v4_scr[block, :] = vf_scr[every, :]
        for res in range(dil):
            for c in range(sub_len // DSW_CHUNK):
                q_start = c * DSW_CHUNK
                k_start = min(max(q_start - DSW_RADIUS, 0), sub_len - win)
                if dil == 1:
                    q = q_ref[q_start:q_start + DSW_CHUNK, :]
                    kw = k_ref[k_start:k_start + win, :]
                    vw = v_ref[k_start:k_start + win, :]
                    rows = pl.ds(q_start, DSW_CHUNK)
                else:
                    if two_hop:
                        rows = krows = pl.ds((res % FAST_STRIDE) * quarter + res // FAST_STRIDE, DSW_CHUNK,
                                             stride=inner)
                    else:
                        rows = pl.ds(res + dil * q_start, DSW_CHUNK, stride=dil)
                        krows = pl.ds(res + dil * k_start, win, stride=dil)
                    q = q_src[rows, :].astype(BF16)
                    kw = k_src[krows, :].astype(BF16)
                    vw = v_src[krows, :].astype(BF16)
                zq = jnp.zeros_like(q)
                q2 = jnp.concatenate([jnp.where(qk_lo, q, zq), jnp.where(qk_lo, zq, q)], axis=0)
                s = lax.dot_general(q2, kw, nt, preferred_element_type=F32)
                band = band_mask(q_start, k_start, win)
                e = jnp.where(jnp.concatenate([band, band], axis=0), jnp.exp2(s), 0.0).astype(BF16)
                one = jnp.ones_like(vw)
                r_a = jnp.dot(e[:DSW_CHUNK], jnp.where(lo, vw, one), preferred_element_type=F32)
                r_b = jnp.dot(e[DSW_CHUNK:], jnp.where(lo, one, vw), preferred_element_type=F32)
                if g == 0 or two_hop:
                    out_scr[0, rows, :] = r_a
                    out_scr[1, rows, :] = r_b
                else:
                    acc_scr[0, rows, :] = acc_scr[0, rows, :] + r_a
                    acc_scr[1, rows, :] = acc_scr[1, rows, :] + r_b
        if two_hop:
            for r in range(FAST_STRIDE):
                block, every = slice(r * quarter, (r + 1) * quarter), pl.ds(r, quarter, stride=FAST_STRIDE)
                acc_scr[0, every, :] = acc_scr[0, every, :] + y4_scr[0, block, :]
                acc_scr[1, every, :] = acc_scr[1, every, :] + y4_scr[1, block, :]

    def combine_unshifted(i, carry):
        d_min, d_max = carry
        r = pl.ds(pl.multiple_of(i * comb_rows, comb_rows), comb_rows)
        acc_a, acc_b = acc_scr[0, r, :], acc_scr[1, r, :]
        num = jnp.where(lo, acc_a, acc_b)
        den = pltpu.roll(jnp.where(lo, acc_b, acc_a), HEAD_DIM, 1)
        z = za_ref[r, :].astype(F32)
        o_ref[r, :] = (num / den * (z * _sigmoid(z))).astype(BF16)
        return (jnp.minimum(d_min, jnp.min(den, axis=0, keepdims=True)),
                jnp.maximum(d_max, jnp.max(den, axis=0, keepdims=True)))

    d_min, d_max = lax.fori_loop(
        0, SEQ // comb_rows, combine_unshifted,
        (jnp.full((1, LANES), SAFE_DEN_MAX, F32), jnp.full((1, LANES), SAFE_DEN_MIN, F32)), unroll=True)
    in_range = (jnp.min(d_min) >= SAFE_DEN_MIN) & (jnp.max(d_max) <= SAFE_DEN_MAX)

    @pl.when(jnp.logical_not(in_range))
    def _():
        for g, (q_ref, k_ref, v_ref) in enumerate(groups):
            dil, sub_len, win = geometry(g)
            n_c = sub_len // DSW_CHUNK
            qf_scr[...] = q_ref[...].astype(F32)
            kf_scr[...] = k_ref[...].astype(F32)
            vf_scr[...] = v_ref[...].astype(F32)

            def chunk(t, carry, g=g, dil=dil, sub_len=sub_len, win=win, n_c=n_c):
                res = t // n_c
                q_start = (t % n_c) * DSW_CHUNK
                k_start = jnp.clip(q_start - DSW_RADIUS, 0, sub_len - win)
                rows = pl.ds(res + dil * q_start, DSW_CHUNK, stride=dil)
                krows = pl.ds(res + dil * k_start, win, stride=dil)
                q = qf_scr[rows, :].astype(BF16)
                kw = kf_scr[krows, :].astype(BF16)
                vw = vf_scr[krows, :].astype(BF16)
                zero = jnp.zeros_like(kw)
                band = band_mask(q_start, k_start, win)
                s_a = lax.dot_general(q, jnp.where(qk_lo, kw, zero), nt, preferred_element_type=F32)
                s_b = lax.dot_general(q, jnp.where(qk_lo, zero, kw), nt, preferred_element_type=F32)
                s_a = jnp.where(band, s_a, NEG_INF)
                s_b = jnp.where(band, s_b, NEG_INF)
                m_a = jnp.max(s_a, axis=-1, keepdims=True)
                m_b = jnp.max(s_b, axis=-1, keepdims=True)
                e_a = jnp.exp2(s_a - m_a)
                e_b = jnp.exp2(s_b - m_b)
                d_a = jnp.sum(e_a, axis=-1, keepdims=True)
                d_b = jnp.sum(e_b, axis=-1, keepdims=True)
                num = (jnp.dot(e_a.astype(BF16), jnp.where(lo, vw, zero), preferred_element_type=F32)
                       + jnp.dot(e_b.astype(BF16), jnp.where(lo, zero, vw), preferred_element_type=F32))
                num_scr[g, rows, :] = num
                max_scr[g, rows, :] = jnp.where(lo, m_a, m_b)
                den_scr[g, rows, :] = jnp.where(lo, d_a, d_b)
                return carry

            lax.fori_loop(0, dil * n_c, chunk, 0)

        def combine(i, carry):
            r = pl.ds(pl.multiple_of(i * comb_rows, comb_rows), comb_rows)
            m0, m1, m2 = max_scr[0, r, :], max_scr[1, r, :], max_scr[2, r, :]
            m = jnp.maximum(jnp.maximum(m0, m1), m2)
            w0, w1, w2 = jnp.exp2(m0 - m), jnp.exp2(m1 - m), jnp.exp2(m2 - m)
            num = w0 * num_scr[0, r, :] + w1 * num_scr[1, r, :] + w2 * num_scr[2, r, :]
            den = w0 * den_scr[0, r, :] + w1 * den_scr[1, r, :] + w2 * den_scr[2, r, :]
            z = za_ref[r, :].astype(F32)
            o_ref[r, :] = (num / den * (z * _sigmoid(z))).astype(BF16)
            return carry

        lax.fori_loop(0, SEQ // comb_rows, combine, 0)


def _dsw_attention(proj):
    def col(base, g):
        return pl.BlockSpec((SEQ, LANES), lambda b, hp: (b, base + 4 * g + hp))

    in_specs = []
    for g in range(3):
        in_specs += [col(QA_B, g), col(KA_B, g), col(VA_B, g)]
    in_specs.append(pl.BlockSpec((SEQ, LANES), lambda b, hp: (b, ZA_B + hp)))
    return pl.pallas_call(
        _dsw_kernel,
        grid=(BATCH, DSW_HEADS_PER_GROUP // 2),
        in_specs=in_specs,
        out_specs=pl.BlockSpec((SEQ, LANES), lambda b, hp: (b, hp)),
        out_shape=jax.ShapeDtypeStruct((BATCH * SEQ, DSW_HEADS_PER_GROUP * HEAD_DIM), BF16),
        scratch_shapes=[pltpu.VMEM((SEQ, LANES), F32)] * 6
        + [pltpu.VMEM((2, SEQ, LANES), F32)] * 2
        + [pltpu.VMEM((3, SEQ, LANES), F32)] * 3,
        compiler_params=pltpu.CompilerParams(dimension_semantics=("arbitrary", "arbitrary"),
                                             vmem_limit_bytes=VMEM_LIMIT),
        name="dsw_attn",
    )(*([proj] * 10))


def _diff_kernel(q_ref, k_ref, v_ref, z_ref, lq1_ref, lk1_ref, lq2_ref, lk2_ref, subln_ref, o_ref,
                 vt_scr, e_scr, d_scr):
    lane = lax.broadcasted_iota(jnp.int32, (1, LANES), 1)
    lo = (lane % HEAD_DIM) < HEAD_DIM // 2
    vt_scr[...] = v_ref[...].T
    lam = (jnp.exp(jnp.sum(lq1_ref[...] * lk1_ref[...], axis=-1, keepdims=True))
           - jnp.exp(jnp.sum(lq2_ref[...] * lk2_ref[...], axis=-1, keepdims=True)) + LAMBDA_INIT)
    nt = (((1,), (1,)), ((), ()))
    n_kt = SEQ // DIFF_TK
    n_chunks = SEQ // DIFF_TQ
    sublanes = d_scr.shape[2]

    def rows_of(c):
        start = c * DIFF_TQ
        return pl.ds(start if isinstance(c, int) else pl.multiple_of(start, DIFF_TQ), DIFF_TQ)

    def masked_halves(q):
        zero = jnp.zeros_like(q)
        return jnp.where(lo, q, zero), jnp.where(lo, zero, q)

    def sublane_partial_sum(e, acc):
        for r in range(e.shape[0] // sublanes):
            part = e[r * sublanes:(r + 1) * sublanes, :]
            acc = part if acc is None else acc + part
        return acc

    def weights_unshifted(c, slot):
        for half, q_half in enumerate(masked_halves(q_ref[rows_of(c), :])):
            s = lax.dot_general(k_ref[...], q_half, nt, preferred_element_type=F32)
            den = None
            for kt in range(n_kt):
                ks = slice(kt * DIFF_TK, (kt + 1) * DIFF_TK)
                e = jnp.exp2(s[ks, :])
                e_scr[slot, half, ks, :] = e.astype(BF16)
                den = sublane_partial_sum(e, den)
            d_scr[slot, half] = den

    def weights_max_shifted(c, slot):
        for half, q_half in enumerate(masked_halves(q_ref[rows_of(c), :])):
            s = lax.dot_general(k_ref[...], q_half, nt, preferred_element_type=F32)
            e = jnp.exp2(s - jnp.max(s, axis=0, keepdims=True))
            e_scr[slot, half] = e.astype(BF16)
            d_scr[slot, half] = sublane_partial_sum(e, None)

    def normalise_pv(c, slot):
        den_a = jnp.sum(d_scr[slot, 0], axis=0, keepdims=True)
        den_b = jnp.sum(d_scr[slot, 1], axis=0, keepdims=True)
        inv_a = (1.0 / den_a).astype(BF16)
        inv_b = (lam / den_b).astype(BF16)
        ot = None
        for kt in range(n_kt):
            ks = slice(kt * DIFF_TK, (kt + 1) * DIFF_TK)
            a = e_scr[slot, 0, ks, :] * inv_a - e_scr[slot, 1, ks, :] * inv_b
            pv = jnp.dot(vt_scr[:, ks], a, preferred_element_type=F32)
            ot = pv if ot is None else ot + pv
        o = ot.T
        y = o * lax.rsqrt(jnp.mean(o * o, axis=-1, keepdims=True) + EPS) * subln_ref[...]
        y = y * (1.0 - LAMBDA_INIT)
        z = z_ref[rows_of(c), :].astype(F32)
        o_ref[rows_of(c), :] = (y * (z * _sigmoid(z))).astype(BF16)
        return (jnp.min(jnp.minimum(den_a, den_b), axis=1, keepdims=True),
                jnp.max(jnp.maximum(den_a, den_b), axis=1, keepdims=True))

    weights_unshifted(0, 0)

    carry = (jnp.full((1, 1), SAFE_DEN_MAX, F32), jnp.full((1, 1), SAFE_DEN_MIN, F32))
    for c0 in range(0, n_chunks, 2):
        weights_unshifted(c0 + 1, 1)
        lo0, hi0 = normalise_pv(c0, 0)
        if c0 + 2 < n_chunks:
            weights_unshifted(c0 + 2, 0)
        lo1, hi1 = normalise_pv(c0 + 1, 1)
        carry = (jnp.minimum(carry[0], jnp.minimum(lo0, lo1)), jnp.maximum(carry[1], jnp.maximum(hi0, hi1)))
    d_min, d_max = carry
    in_range = (jnp.min(d_min) >= SAFE_DEN_MIN) & (jnp.max(d_max) <= SAFE_DEN_MAX)

    @pl.when(jnp.logical_not(in_range))
    def _():
        def body(c, carry):
            weights_max_shifted(c, 0)
            normalise_pv(c, 0)
            return carry

        lax.fori_loop(0, n_chunks, body, 0)


def _diff_attention(proj, lq1, lk1, lq2, lk2, subln):
    def col(base):
        return pl.BlockSpec((SEQ, LANES), lambda b, h: (b, base + h))

    small = pl.BlockSpec((1, HEAD_DIM), lambda b, h: (0, 0))
    return pl.pallas_call(
        _diff_kernel,
        grid=(BATCH, DIFF_HEADS),
        in_specs=[col(QD_B), col(KD_B), col(VD_B), col(ZD_B), small, small, small, small,
                  pl.BlockSpec((1, LANES), lambda b, h: (0, 0))],
        out_specs=pl.BlockSpec((SEQ, LANES), lambda b, h: (b, h)),
        out_shape=jax.ShapeDtypeStruct((BATCH * SEQ, DIFF_HEADS * LANES), BF16),
        scratch_shapes=[pltpu.VMEM((LANES, SEQ), BF16),
                        pltpu.VMEM((2, 2, SEQ, DIFF_TQ), BF16),
                        pltpu.VMEM((2, 2, SUBLANES, DIFF_TQ), F32)],
        compiler_params=pltpu.CompilerParams(dimension_semantics=("arbitrary", "arbitrary"),
                                             vmem_limit_bytes=VMEM_LIMIT),
        name="diff_attn",
    )(proj, proj, proj, proj, lq1, lk1, lq2, lk2, subln)


def _out_kernel(ya_ref, yd_ref, ga_ref, gb_ref, x_ref, gate_ref, wpa_ref, wpb_ref, wout_ref, npost_ref,
                o_ref):
    pa = jnp.dot(ya_ref[...], wpa_ref[...], preferred_element_type=F32)
    pb = jnp.dot(yd_ref[...], wpb_ref[...], preferred_element_type=F32)
    mix = _sigmoid(ga_ref[...].astype(F32)) * pa + _sigmoid(gb_ref[...].astype(F32)) * pb
    y = jnp.dot(mix.astype(BF16), wout_ref[...], preferred_element_type=F32)
    y = y * lax.rsqrt(jnp.mean(y * y, axis=-1, keepdims=True) + EPS) * npost_ref[...]
    o_ref[...] = x_ref[...] + gate_ref[...] * y


def _out_stage(ya, yd, proj, x2d, mod4, wpa, wpb, wout, norm_post):
    per_batch = SEQ // OUT_TM
    gm_block = GM_COL // D_MODEL
    const = lambda i: (0, 0)
    return pl.pallas_call(
        _out_kernel,
        grid=(BATCH * per_batch,),
        in_specs=[pl.BlockSpec((OUT_TM, DSW_HEADS_PER_GROUP * HEAD_DIM), lambda i: (i, 0)),
                  pl.BlockSpec((OUT_TM, D_MODEL), lambda i: (i, 0)),
                  pl.BlockSpec((OUT_TM, D_MODEL), lambda i: (i, gm_block)),
                  pl.BlockSpec((OUT_TM, D_MODEL), lambda i: (i, gm_block + 1)),
                  pl.BlockSpec((OUT_TM, D_MODEL), lambda i: (i, 0)),
                  pl.BlockSpec((None, None, 1, D_MODEL), lambda i: (i // per_batch, 2, 0, 0)),
                  pl.BlockSpec((DSW_HEADS_PER_GROUP * HEAD_DIM, D_MODEL), const),
                  pl.BlockSpec((D_MODEL, D_MODEL), const),
                  pl.BlockSpec((D_MODEL, D_MODEL), const),
                  pl.BlockSpec((1, D_MODEL), const)],
        out_specs=pl.BlockSpec((OUT_TM, D_MODEL), lambda i: (i, 0)),
        out_shape=jax.ShapeDtypeStruct((BATCH * SEQ, D_MODEL), F32),
        compiler_params=pltpu.CompilerParams(dimension_semantics=("arbitrary",),
                                             vmem_limit_bytes=VMEM_LIMIT),
        name="out_stage",
    )(ya, yd, proj, proj, x2d, mod4, wpa, wpb, wout, norm_post)


def kernel(x, c, positions, w_ada, b_ada, norm_pre, w_in, lambda_q1, lambda_k1, lambda_q2, lambda_k2,
           diff_subln, w_proj_a, w_proj_b, w_out, norm_post):
    assert x.shape == (BATCH, SEQ, D_MODEL) and w_in.shape == (1, D_MODEL, IN_WIDTH)
    x2d = x.reshape(BATCH * SEQ, D_MODEL)
    pos2d = jnp.repeat(positions.reshape(BATCH * SEQ // ROPE_PACK, ROPE_PACK), HEAD_DIM // 2, axis=1)
    half = HEAD_DIM // 2
    inv_freq = ROPE_THETA ** (-jnp.arange(half, dtype=F32) / half)
    inv_freq = jnp.tile(inv_freq, LANES // half).reshape(1, LANES)

    mod = _adaln(c, w_ada[0], b_ada)
    mod4 = mod.reshape(BATCH, 3, 1, D_MODEL)
    col = jnp.arange(IN_WIDTH)
    is_q_col = (col < KA_COL) | ((col >= QD_COL) & (col < KD_COL))
    col_scale = jnp.where(is_q_col, Q_SCALE, 1.0).astype(F32).reshape(1, IN_WIDTH)
    proj = _inproj(x2d, pos2d, mod4, norm_pre, inv_freq, _prep_w_in(w_in[0]), col_scale)
    ya = _dsw_attention(proj)
    yd = _diff_attention(proj, lambda_q1, lambda_k1, lambda_q2, lambda_k2, diff_subln)
    out = _out_stage(ya, yd, proj, x2d, mod4, w_proj_a[0].astype(BF16), w_proj_b[0].astype(BF16),
                     w_out[0].astype(BF16), norm_post)
    return out.reshape(BATCH, SEQ, D_MODEL)
```

```python
import math

import jax
import jax.numpy as jnp
from jax import lax
from jax.experimental import pallas as pl
from jax.experimental.pallas import tpu as pltpu

F32 = jnp.float32
BF16 = jnp.bfloat16

D_MODEL = 1024
BATCH = 16
SEQ = 2048
HEAD_DIM = 64
ROPE_THETA = 10000.0
EPS = 1e-6
NEG_INF = -1e30
DSW_DILATIONS = (1, 4, 16)
DSW_RADIUS = 64
DSW_HEADS_PER_GROUP = 8
DIFF_HEADS = 8
LAMBDA_INIT = 0.8 - 0.6 * math.exp(-0.3 * 0)

LANES = 128
SUBLANES = 8
IN_TILE_N = 1024
IN_CHUNK_M = 128
ROPE_PACK = LANES // (HEAD_DIM // 2)
Q_SCALE = HEAD_DIM ** -0.5 * math.log2(math.e)
QA_COL, KA_COL, VA_COL, ZA_COL, QD_COL, KD_COL, VD_COL, ZD_COL, GM_COL, IN_WIDTH = (
    0, 1536, 3072, 4608, 5120, 6144, 7168, 8192, 9216, 11264)
IN_FIRST_PLAIN_TILE = VA_COL // IN_TILE_N
assert VA_COL % IN_TILE_N == 0 and QD_COL % IN_TILE_N == 0 and VD_COL % IN_TILE_N == 0
QA_B, KA_B, VA_B, ZA_B, QD_B, KD_B, VD_B, ZD_B = (
    c // LANES for c in (QA_COL, KA_COL, VA_COL, ZA_COL, QD_COL, KD_COL, VD_COL, ZD_COL))
DSW_CHUNK = 128
FAST_STRIDE = 4
DIFF_TQ = 512
DIFF_TK = 256
DIFF_HEADS_PER_STEP = 2
SAFE_DEN_MIN = 2.0 ** -90
SAFE_DEN_MAX = 2.0 ** 115
OUT_TM = 1024
VMEM_LIMIT = 56 * 1024 * 1024


def _sigmoid(z):
    return 1.0 / (1.0 + jnp.exp(-z))


def _adaln_kernel(c_ref, w_ref, b_ref, o_ref):
    c = c_ref[...]
    sc = c * _sigmoid(c)
    o_ref[...] = jnp.dot(sc, w_ref[...], preferred_element_type=F32,
                         precision=lax.Precision.HIGHEST) + b_ref[...]


def _adaln(c, w_ada, b_ada):
    n_tiles = 3
    return pl.pallas_call(
        _adaln_kernel,
        grid=(n_tiles,),
        in_specs=[pl.BlockSpec((BATCH, D_MODEL), lambda n: (0, 0)),
                  pl.BlockSpec((D_MODEL, D_MODEL), lambda n: (0, n)),
                  pl.BlockSpec((1, D_MODEL), lambda n: (0, n))],
        out_specs=pl.BlockSpec((BATCH, D_MODEL), lambda n: (0, n)),
        out_shape=jax.ShapeDtypeStruct((BATCH, 3 * D_MODEL), F32),
        compiler_params=pltpu.CompilerParams(dimension_semantics=("arbitrary",),
                                             vmem_limit_bytes=VMEM_LIMIT),
        name="adaln",
    )(c, w_ada, b_ada)


def _is_rope_tile(tile):
    return (tile < IN_FIRST_PLAIN_TILE) | ((tile >= QD_COL // IN_TILE_N) & (tile < VD_COL // IN_TILE_N))


def _wprep_kernel(w_ref, o_ref):
    @pl.when(_is_rope_tile(pl.program_id(0)))
    def _():
        group = lax.broadcasted_iota(jnp.int32, (1, LANES), 1) // (HEAD_DIM // 2)
        for j in range(IN_TILE_N // LANES):
            a = w_ref[:, j * LANES:(j + 1) * LANES]
            swapped = jnp.where(group == 1, pltpu.roll(a, LANES - HEAD_DIM // 2, 1),
                                jnp.where(group == 2, pltpu.roll(a, HEAD_DIM // 2, 1), a))
            o_ref[:, j * LANES:(j + 1) * LANES] = swapped.astype(BF16)

    @pl.when(jnp.logical_not(_is_rope_tile(pl.program_id(0))))
    def _():
        o_ref[...] = w_ref[...].astype(BF16)


def _prep_w_in(w_in):
    return pl.pallas_call(
        _wprep_kernel,
        grid=(IN_WIDTH // IN_TILE_N,),
        in_specs=[pl.BlockSpec((D_MODEL, IN_TILE_N), lambda t: (0, t))],
        out_specs=pl.BlockSpec((D_MODEL, IN_TILE_N), lambda t: (0, t)),
        out_shape=jax.ShapeDtypeStruct((D_MODEL, IN_WIDTH), BF16),
        compiler_params=pltpu.CompilerParams(dimension_semantics=("arbitrary",),
                                             vmem_limit_bytes=VMEM_LIMIT),
        name="wprep",
    )(w_in)


def _inproj_tile(n):
    return jnp.where(n < 2, n + IN_FIRST_PLAIN_TILE, jnp.where(n < 2 + IN_FIRST_PLAIN_TILE, n - 2, n))


def _inproj_kernel(x_ref, pos_ref, shift_ref, scale_ref, npre_ref, invf_ref, w_ref, cs_ref, o_ref,
                   h_scr, cos_scr, sin_scr):
    n = pl.program_id(1)
    tile = _inproj_tile(n)
    is_rope = (n >= 2) & _is_rope_tile(tile)

    def chunks(before, epilogue):
        for c in range(SEQ // IN_CHUNK_M):
            rows = slice(c * IN_CHUNK_M, (c + 1) * IN_CHUNK_M)
            before(rows)
            acc = jnp.dot(h_scr[rows, :], w_ref[...], preferred_element_type=F32)
            epilogue(rows, acc)

    def nothing(rows):
        pass

    def plain(rows, acc):
        o_ref[rows, :] = acc.astype(BF16)

    def normalise_rows(rows):
        x = x_ref[rows, :]
        ms = jnp.mean(x * x, axis=-1, keepdims=True)
        y = x * lax.rsqrt(ms + EPS) * npre_ref[...]
        h_scr[rows, :] = (y * (1.0 + scale_ref[...]) + shift_ref[...]).astype(BF16)

    def rotary_tables(rows):
        n_packed = IN_CHUNK_M // ROPE_PACK
        ang = pos_ref[rows.start // ROPE_PACK:rows.start // ROPE_PACK + n_packed, :].astype(F32) * invf_ref[...]
        lane = lax.broadcasted_iota(jnp.int32, (1, LANES), 1)
        sgn = jnp.where(lane < LANES // 2, -1.0, 1.0)
        group = lane // (HEAD_DIM // 2)
        for table, dst, sign in ((jnp.cos(ang), cos_scr, None), (jnp.sin(ang), sin_scr, sgn)):
            for j in range(ROPE_PACK):
                own = jnp.where(group == j, table, 0.0)
                rep = own
                for shift in range(1, ROPE_PACK):
                    rep = rep + pltpu.roll(own, shift * (HEAD_DIM // 2), 1)
                if sign is not None:
                    rep = rep * sign
                dst[pl.ds(rows.start + j, n_packed, stride=ROPE_PACK), :] = rep

    def rope(rows, acc):
        cos, sin = cos_scr[rows, :], sin_scr[rows, :]
        for j in range(IN_TILE_N // LANES):
            a = acc[:, j * LANES:(j + 1) * LANES] * cs_ref[:, j * LANES:(j + 1) * LANES]
            o_ref[rows, j * LANES:(j + 1) * LANES] = (
                a * cos + pltpu.roll(a, LANES // 2, 1) * sin).astype(BF16)

    pl.when(n == 0)(lambda: chunks(normalise_rows, plain))
    pl.when(n == 1)(lambda: chunks(rotary_tables, plain))
    pl.when(is_rope)(lambda: chunks(nothing, rope))
    pl.when((n >= 2) & jnp.logical_not(is_rope))(lambda: chunks(nothing, plain))


def _inproj(x2d, pos2d, mod4, norm_pre, inv_freq, w_in_bf16, col_scale):
    return pl.pallas_call(
        _inproj_kernel,
        grid=(BATCH, IN_WIDTH // IN_TILE_N),
        in_specs=[pl.BlockSpec((SEQ, D_MODEL), lambda b, n: (b, 0)),
                  pl.BlockSpec((SEQ // ROPE_PACK, LANES), lambda b, n: (b, 0)),
                  pl.BlockSpec((None, None, 1, D_MODEL), lambda b, n: (b, 0, 0, 0)),
                  pl.BlockSpec((None, None, 1, D_MODEL), lambda b, n: (b, 1, 0, 0)),
                  pl.BlockSpec((1, D_MODEL), lambda b, n: (0, 0)),
                  pl.BlockSpec((1, LANES), lambda b, n: (0, 0)),
                  pl.BlockSpec((D_MODEL, IN_TILE_N), lambda b, n: (0, _inproj_tile(n))),
                  pl.BlockSpec((1, IN_TILE_N), lambda b, n: (0, _inproj_tile(n)))],
        out_specs=pl.BlockSpec((SEQ, IN_TILE_N), lambda b, n: (b, _inproj_tile(n))),
        out_shape=jax.ShapeDtypeStruct((BATCH * SEQ, IN_WIDTH), BF16),
        scratch_shapes=[pltpu.VMEM((SEQ, D_MODEL), BF16),
                        pltpu.VMEM((SEQ, LANES), F32),
                        pltpu.VMEM((SEQ, LANES), F32)],
        compiler_params=pltpu.CompilerParams(dimension_semantics=("arbitrary", "arbitrary"),
                                             vmem_limit_bytes=VMEM_LIMIT),
        name="inproj",
    )(x2d, pos2d, mod4, mod4, norm_pre, inv_freq, w_in_bf16, col_scale)


def _dsw_kernel(q0_ref, k0_ref, v0_ref, q1_ref, k1_ref, v1_ref, q2_ref, k2_ref, v2_ref,
                za_ref, o_ref, qf_scr, kf_scr, vf_scr, q4_scr, k4_scr, v4_scr, acc_scr, y4_scr,
                num_scr, max_scr, den_scr):
    lane = lax.broadcasted_iota(jnp.int32, (1, LANES), 1)
    lo = lane < HEAD_DIM
    qk_lo = (lane % HEAD_DIM) < HEAD_DIM // 2
    groups = ((q0_ref, k0_ref, v0_ref), (q1_ref, k1_ref, v1_ref), (q2_ref, k2_ref, v2_ref))
    nt = (((1,), (1,)), ((), ()))
    comb_rows = 256

    def geometry(g):
        dil = DSW_DILATIONS[g]
        sub_len = SEQ // dil
        win = min(2 * DSW_CHUNK, sub_len)
        return dil, sub_len, win

    def band_mask(q_start, k_start, win):
        kj_minus_qi = (lax.broadcasted_iota(jnp.int32, (DSW_CHUNK, win), 1)
                       - lax.broadcasted_iota(jnp.int32, (DSW_CHUNK, win), 0))
        return jnp.abs(kj_minus_qi + (k_start - q_start)) <= DSW_RADIUS

    for g, (q_ref, k_ref, v_ref) in enumerate(groups):
        dil, sub_len, win = geometry(g)
        two_hop = dil > FAST_STRIDE
        inner = dil // FAST_STRIDE
        quarter = SEQ // FAST_STRIDE
        q_src, k_src, v_src, out_scr = qf_scr, kf_scr, vf_scr, acc_scr
        if dil > 1:
            qf_scr[...] = q_ref[...].astype(F32)
            kf_scr[...] = k_ref[...].astype(F32)
            vf_scr[...] = v_ref[...].astype(F32)
        if two_hop:
            assert sub_len == DSW_CHUNK
            q_src, k_src, v_src, out_scr = q4_scr, k4_scr, v4_scr, y4_scr
            for r in range(FAST_STRIDE):
                block, every = slice(r * quarter, (r + 1) * quarter), pl.ds(r, quarter, stride=FAST_STRIDE)
                q4_scr[block, :] = qf_scr[every, :]
                k4_scr[block, :] = kf_scr[every, :]
                v4_scr[block, :] = vf_scr[every, :]
        for res in range(dil):
            for c in range(sub_len // DSW_CHUNK):
                q_start = c * DSW_CHUNK
                k_start = min(max(q_start - DSW_RADIUS, 0), sub_len - win)
                if dil == 1:
                    q = q_ref[q_start:q_start + DSW_CHUNK, :]
                    kw = k_ref[k_start:k_start + win, :]
                    vw = v_ref[k_start:k_start + win, :]
                    rows = pl.ds(q_start, DSW_CHUNK)
                else:
                    if two_hop:
                        rows = krows = pl.ds((res % FAST_STRIDE) * quarter + res // FAST_STRIDE, DSW_CHUNK,
                                             stride=inner)
                    else:
                        rows = pl.ds(res + dil * q_start, DSW_CHUNK, stride=dil)
                        krows = pl.ds(res + dil * k_start, win, stride=dil)
                    q = q_src[rows, :].astype(BF16)
                    kw = k_src[krows, :].astype(BF16)
                    vw = v_src[krows, :].astype(BF16)
                zq = jnp.zeros_like(q)
                q2 = jnp.concatenate([jnp.where(qk_lo, q, zq), jnp.where(qk_lo, zq, q)], axis=0)
                s = lax.dot_general(q2, kw, nt, preferred_element_type=F32)
                band = band_mask(q_start, k_start, win)
                e = jnp.where(jnp.concatenate([band, band], axis=0), jnp.exp2(s), 0.0).astype(BF16)
                one = jnp.ones_like(vw)
                r_a = jnp.dot(e[:DSW_CHUNK], jnp.where(lo, vw, one), preferred_element_type=F32)
                r_b = jnp.dot(e[DSW_CHUNK:], jnp.where(lo, one, vw), preferred_element_type=F32)
                if g == 0 or two_hop:
                    out_scr[0, rows, :] = r_a
                    out_scr[1, rows, :] = r_b
                else:
                    acc_scr[0, rows, :] = acc_scr[0, rows, :] + r_a
                    acc_scr[1, rows, :] = acc_scr[1, rows, :] + r_b
        if two_hop:
            for r in range(FAST_STRIDE):
                block, every = slice(r * quarter, (r + 1) * quarter), pl.ds(r, quarter, stride=FAST_STRIDE)
                acc_scr[0, every, :] = acc_scr[0, every, :] + y4_scr[0, block, :]
                acc_scr[1, every, :] = acc_scr[1, every, :] + y4_scr[1, block, :]

    def combine_unshifted(i, carry):
        d_min, d_max = carry
        r = pl.ds(pl.multiple_of(i * comb_rows, comb_rows), comb_rows)
        acc_a, acc_b = acc_scr[0, r, :], acc_scr[1, r, :]
        num = jnp.where(lo, acc_a, acc_b)
        den = pltpu.roll(jnp.where(lo, acc_b, acc_a), HEAD_DIM, 1)
        z = za_ref[r, :].astype(F32)
        o_ref[r, :] = (num / den * (z * _sigmoid(z))).astype(BF16)
        return (jnp.minimum(d_min, jnp.min(den, axis=0, keepdims=True)),
                jnp.maximum(d_max, jnp.max(den, axis=0, keepdims=True)))

    d_min, d_max = lax.fori_loop(
        0, SEQ // comb_rows, combine_unshifted,
        (jnp.full((1, LANES), SAFE_DEN_MAX, F32), jnp.full((1, LANES), SAFE_DEN_MIN, F32)), unroll=True)
    in_range = (jnp.min(d_min) >= SAFE_DEN_MIN) & (jnp.max(d_max) <= SAFE_DEN_MAX)

    @pl.when(jnp.logical_not(in_range))
    def _():
        for g, (q_ref, k_ref, v_ref) in enumerate(groups):
            dil, sub_len, win = geometry(g)
            n_c = sub_len // DSW_CHUNK
            qf_scr[...] = q_ref[...].astype(F32)
            kf_scr[...] = k_ref[...].astype(F32)
            vf_scr[...] = v_ref[...].astype(F32)

            def chunk(t, carry, g=g, dil=dil, sub_len=sub_len, win=win, n_c=n_c):
                res = t // n_c
                q_start = (t % n_c) * DSW_CHUNK
                k_start = jnp.clip(q_start - DSW_RADIUS, 0, sub_len - win)
                rows = pl.ds(res + dil * q_start, DSW_CHUNK, stride=dil)
                krows = pl.ds(res + dil * k_start, win, stride=dil)
                q = qf_scr[rows, :].astype(BF16)
                kw = kf_scr[krows, :].astype(BF16)
                vw = vf_scr[krows, :].astype(BF16)
                zero = jnp.zeros_like(kw)
                band = band_mask(q_start, k_start, win)
                s_a = lax.dot_general(q, jnp.where(qk_lo, kw, zero), nt, preferred_element_type=F32)
                s_b = lax.dot_general(q, jnp.where(qk_lo, zero, kw), nt, preferred_element_type=F32)
                s_a = jnp.where(band, s_a, NEG_INF)
                s_b = jnp.where(band, s_b, NEG_INF)
                m_a = jnp.max(s_a, axis=-1, keepdims=True)
                m_b = jnp.max(s_b, axis=-1, keepdims=True)
                e_a = jnp.exp2(s_a - m_a)
                e_b = jnp.exp2(s_b - m_b)
                d_a = jnp.sum(e_a, axis=-1, keepdims=True)
                d_b = jnp.sum(e_b, axis=-1, keepdims=True)
                num = (jnp.dot(e_a.astype(BF16), jnp.where(lo, vw, zero), preferred_element_type=F32)
                       + jnp.dot(e_b.astype(BF16), jnp.where(lo, zero, vw), preferred_element_type=F32))
                num_scr[g, rows, :] = num
                max_scr[g, rows, :] = jnp.where(lo, m_a, m_b)
                den_scr[g, rows, :] = jnp.where(lo, d_a, d_b)
                return carry

            lax.fori_loop(0, dil * n_c, chunk, 0)

        def combine(i, carry):
            r = pl.ds(pl.multiple_of(i * comb_rows, comb_rows), comb_rows)
            m0, m1, m2 = max_scr[0, r, :], max_scr[1, r, :], max_scr[2, r, :]
            m = jnp.maximum(jnp.maximum(m0, m1), m2)
            w0, w1, w2 = jnp.exp2(m0 - m), jnp.exp2(m1 - m), jnp.exp2(m2 - m)
            num = w0 * num_scr[0, r, :] + w1 * num_scr[1, r, :] + w2 * num_scr[2, r, :]
            den = w0 * den_scr[0, r, :] + w1 * den_scr[1, r, :] + w2 * den_scr[2, r, :]
            z = za_ref[r, :].astype(F32)
            o_ref[r, :] = (num / den * (z * _sigmoid(z))).astype(BF16)
            return carry

        lax.fori_loop(0, SEQ // comb_rows, combine, 0)


def _dsw_attention(proj):
    def col(base, g):
        return pl.BlockSpec((SEQ, LANES), lambda b, hp: (b, base + 4 * g + hp))

    in_specs = []
    for g in range(3):
        in_specs += [col(QA_B, g), col(KA_B, g), col(VA_B, g)]
    in_specs.append(pl.BlockSpec((SEQ, LANES), lambda b, hp: (b, ZA_B + hp)))
    return pl.pallas_call(
        _dsw_kernel,
        grid=(BATCH, DSW_HEADS_PER_GROUP // 2),
        in_specs=in_specs,
        out_specs=pl.BlockSpec((SEQ, LANES), lambda b, hp: (b, hp)),
        out_shape=jax.ShapeDtypeStruct((BATCH * SEQ, DSW_HEADS_PER_GROUP * HEAD_DIM), BF16),
        scratch_shapes=[pltpu.VMEM((SEQ, LANES), F32)] * 6
        + [pltpu.VMEM((2, SEQ, LANES), F32)] * 2
        + [pltpu.VMEM((3, SEQ, LANES), F32)] * 3,
        compiler_params=pltpu.CompilerParams(dimension_semantics=("arbitrary", "arbitrary"),
                                             vmem_limit_bytes=VMEM_LIMIT),
        name="dsw_attn",
    )(*([proj] * 10))


def _diff_kernel(q_ref, k_ref, v_ref, z_ref, lq1_ref, lk1_ref, lq2_ref, lk2_ref, subln_ref, o_ref,
                 vt_scr, e_scr, d_scr):
    lane = lax.broadcasted_iota(jnp.int32, (1, LANES), 1)
    lo = (lane % HEAD_DIM) < HEAD_DIM // 2

    def lanes_of(h):
        return slice(h * LANES, (h + 1) * LANES)

    for h in range(DIFF_HEADS_PER_STEP):
        vt_scr[h] = v_ref[:, lanes_of(h)].T
    lam = (jnp.exp(jnp.sum(lq1_ref[...] * lk1_ref[...], axis=-1, keepdims=True))
           - jnp.exp(jnp.sum(lq2_ref[...] * lk2_ref[...], axis=-1, keepdims=True)) + LAMBDA_INIT)
    nt = (((1,), (1,)), ((), ()))
    n_kt = SEQ // DIFF_TK
    n_chunks = SEQ // DIFF_TQ
    sublanes = d_scr.shape[2]

    def rows_of(c):
        start = c * DIFF_TQ
        return pl.ds(start if isinstance(c, int) else pl.multiple_of(start, DIFF_TQ), DIFF_TQ)

    def masked_halves(q):
        zero = jnp.zeros_like(q)
        return jnp.where(lo, q, zero), jnp.where(lo, zero, q)

    def sublane_partial_sum(e, acc):
        for r in range(e.shape[0] // sublanes):
            part = e[r * sublanes:(r + 1) * sublanes, :]
            acc = part if acc is None else acc + part
        return acc

    def weights_unshifted(h, c, slot):
        for half, q_half in enumerate(masked_halves(q_ref[rows_of(c), lanes_of(h)])):
            s = lax.dot_general(k_ref[:, lanes_of(h)], q_half, nt, preferred_element_type=F32)
            den = None
            for kt in range(n_kt):
                ks = slice(kt * DIFF_TK, (kt + 1) * DIFF_TK)
                e = jnp.exp2(s[ks, :])
                e_scr[slot, half, ks, :] = e.astype(BF16)
                den = sublane_partial_sum(e, den)
            d_scr[slot, half] = den

    def weights_max_shifted(h, c, slot):
        for half, q_half in enumerate(masked_halves(q_ref[rows_of(c), lanes_of(h)])):
            s = lax.dot_general(k_ref[:, lanes_of(h)], q_half, nt, preferred_element_type=F32)
            e = jnp.exp2(s - jnp.max(s, axis=0, keepdims=True))
            e_scr[slot, half] = e.astype(BF16)
            d_scr[slot, half] = sublane_partial_sum(e, None)

    def normalise_pv(h, c, slot):
        den_a = jnp.sum(d_scr[slot, 0], axis=0, keepdims=True)
        den_b = jnp.sum(d_scr[slot, 1], axis=0, keepdims=True)
        inv_a = (1.0 / den_a).astype(BF16)
        inv_b = (lam / den_b).astype(BF16)
        ot = None
        for kt in range(n_kt):
            ks = slice(kt * DIFF_TK, (kt + 1) * DIFF_TK)
            a = e_scr[slot, 0, ks, :] * inv_a - e_scr[slot, 1, ks, :] * inv_b
            pv = jnp.dot(vt_scr[h, :, ks], a, preferred_element_type=F32)
            ot = pv if ot is None else ot + pv
        o = ot.T
        y = o * lax.rsqrt(jnp.mean(o * o, axis=-1, keepdims=True) + EPS) * subln_ref[...]
        y = y * (1.0 - LAMBDA_INIT)
        z = z_ref[rows_of(c), lanes_of(h)].astype(F32)
        o_ref[rows_of(c), lanes_of(h)] = (y * (z * _sigmoid(z))).astype(BF16)
        return (jnp.min(jnp.minimum(den_a, den_b), axis=1, keepdims=True),
                jnp.max(jnp.maximum(den_a, den_b), axis=1, keepdims=True))

    work = [(h, c) for h in range(DIFF_HEADS_PER_STEP) for c in range(n_chunks)]
    stats = {h: [] for h in range(DIFF_HEADS_PER_STEP)}
    weights_unshifted(*work[0], 0)
    for i, (h, c) in enumerate(work):
        if i + 1 < len(work):
            weights_unshifted(*work[i + 1], (i + 1) % 2)
        stats[h].append(normalise_pv(h, c, i % 2))

    for h in range(DIFF_HEADS_PER_STEP):
        d_min, d_max = stats[h][0]
        for lo_i, hi_i in stats[h][1:]:
            d_min, d_max = jnp.minimum(d_min, lo_i), jnp.maximum(d_max, hi_i)
        in_range = (jnp.min(d_min) >= SAFE_DEN_MIN) & (jnp.max(d_max) <= SAFE_DEN_MAX)

        @pl.when(jnp.logical_not(in_range))
        def _(h=h):
            def body(c, carry):
                weights_max_shifted(h, c, 0)
                normalise_pv(h, c, 0)
                return carry

            lax.fori_loop(0, n_chunks, body, 0)


def _diff_attention(proj, lq1, lk1, lq2, lk2, subln):
    width = DIFF_HEADS_PER_STEP * LANES

    def col(base):
        assert base % DIFF_HEADS_PER_STEP == 0
        return pl.BlockSpec((SEQ, width), lambda b, h: (b, base // DIFF_HEADS_PER_STEP + h))

    small = pl.BlockSpec((1, HEAD_DIM), lambda b, h: (0, 0))
    return pl.pallas_call(
        _diff_kernel,
        grid=(BATCH, DIFF_HEADS // DIFF_HEADS_PER_STEP),
        in_specs=[col(QD_B), col(KD_B), col(VD_B), col(ZD_B), small, small, small, small,
                  pl.BlockSpec((1, LANES), lambda b, h: (0, 0))],
        out_specs=pl.BlockSpec((SEQ, width), lambda b, h: (b, h)),
        out_shape=jax.ShapeDtypeStruct((BATCH * SEQ, DIFF_HEADS * LANES), BF16),
        scratch_shapes=[pltpu.VMEM((DIFF_HEADS_PER_STEP, LANES, SEQ), BF16),
                        pltpu.VMEM((2, 2, SEQ, DIFF_TQ), BF16),
                        pltpu.VMEM((2, 2, SUBLANES, DIFF_TQ), F32)],
        compiler_params=pltpu.CompilerParams(dimension_semantics=("arbitrary", "arbitrary"),
                                             vmem_limit_bytes=VMEM_LIMIT),
        name="diff_attn",
    )(proj, proj, proj, proj, lq1, lk1, lq2, lk2, subln)


def _out_kernel(ya_ref, yd_ref, ga_ref, gb_ref, x_ref, gate_ref, wpa_ref, wpb_ref, wout_ref, npost_ref,
                o_ref):
    pa = jnp.dot(ya_ref[...], wpa_ref[...], preferred_element_type=F32)
    pb = jnp.dot(yd_ref[...], wpb_ref[...], preferred_element_type=F32)
    mix = _sigmoid(ga_ref[...].astype(F32)) * pa + _sigmoid(gb_ref[...].astype(F32)) * pb
    y = jnp.dot(mix.astype(BF16), wout_ref[...], preferred_element_type=F32)
    y = y * lax.rsqrt(jnp.mean(y * y, axis=-1, keepdims=True) + EPS) * npost_ref[...]
    o_ref[...] = x_ref[...] + gate_ref[...] * y


def _out_stage(ya, yd, proj, x2d, mod4, wpa, wpb, wout, norm_post):
    per_batch = SEQ // OUT_TM
    gm_block = GM_COL // D_MODEL
    const = lambda i: (0, 0)
    return pl.pallas_call(
        _out_kernel,
        grid=(BATCH * per_batch,),
        in_specs=[pl.BlockSpec((OUT_TM, DSW_HEADS_PER_GROUP * HEAD_DIM), lambda i: (i, 0)),
                  pl.BlockSpec((OUT_TM, D_MODEL), lambda i: (i, 0)),
                  pl.BlockSpec((OUT_TM, D_MODEL), lambda i: (i, gm_block)),
                  pl.BlockSpec((OUT_TM, D_MODEL), lambda i: (i, gm_block + 1)),
                  pl.BlockSpec((OUT_TM, D_MODEL), lambda i: (i, 0)),
                  pl.BlockSpec((None, None, 1, D_MODEL), lambda i: (i // per_batch, 2, 0, 0)),
                  pl.BlockSpec((DSW_HEADS_PER_GROUP * HEAD_DIM, D_MODEL), const),
                  pl.BlockSpec((D_MODEL, D_MODEL), const),
                  pl.BlockSpec((D_MODEL, D_MODEL), const),
                  pl.BlockSpec((1, D_MODEL), const)],
        out_specs=pl.BlockSpec((OUT_TM, D_MODEL), lambda i: (i, 0)),
        out_shape=jax.ShapeDtypeStruct((BATCH * SEQ, D_MODEL), F32),
        compiler_params=pltpu.CompilerParams(dimension_semantics=("arbitrary",),
                                             vmem_limit_bytes=VMEM_LIMIT),
        name="out_stage",
    )(ya, yd, proj, proj, x2d, mod4, wpa, wpb, wout, norm_post)


def kernel(x, c, positions, w_ada, b_ada, norm_pre, w_in, lambda_q1, lambda_k1, lambda_q2, lambda_k2,
           diff_subln, w_proj_a, w_proj_b, w_out, norm_post):
    assert x.shape == (BATCH, SEQ, D_MODEL) and w_in.shape == (1, D_MODEL, IN_WIDTH)
    x2d = x.reshape(BATCH * SEQ, D_MODEL)
    pos2d = jnp.repeat(positions.reshape(BATCH * SEQ // ROPE_PACK, ROPE_PACK), HEAD_DIM // 2, axis=1)
    half = HEAD_DIM // 2
    inv_freq = ROPE_THETA ** (-jnp.arange(half, dtype=F32) / half)
    inv_freq = jnp.tile(inv_freq, LANES // half).reshape(1, LANES)

    mod = _adaln(c, w_ada[0], b_ada)
    mod4 = mod.reshape(BATCH, 3, 1, D_MODEL)
    col = jnp.arange(IN_WIDTH)
    is_q_col = (col < KA_COL) | ((col >= QD_COL) & (col < KD_COL))
    col_scale = jnp.where(is_q_col, Q_SCALE, 1.0).astype(F32).reshape(1, IN_WIDTH)
    proj = _inproj(x2d, pos2d, mod4, norm_pre, inv_freq, _prep_w_in(w_in[0]), col_scale)
    ya = _dsw_attention(proj)
    yd = _diff_attention(proj, lambda_q1, lambda_k1, lambda_q2, lambda_k2, diff_subln)
    out = _out_stage(ya, yd, proj, x2d, mod4, w_proj_a[0].astype(BF16), w_proj_b[0].astype(BF16),
                     w_out[0].astype(BF16), norm_post)
    return out.reshape(BATCH, SEQ, D_MODEL)
```
